```python
import math
import jax, jax.numpy as jnp
from jax import lax
import numpy as np

D_MODEL = 4096
BATCH = 8
SEQ = 2048
DEPTH = 1
DEC_BATCH = 16
DEC_SEQ = 32
PAST_LEN = 4096

CHUNK = 64
D_MIX = D_MODEL
D_MLSTM = D_MIX // 2
D_RGLRU = D_MIX - D_MLSTM
MLSTM_HEADS = 4
MLSTM_DV = D_MLSTM // MLSTM_HEADS
MLSTM_DK = MLSTM_DV // 2
RG_BLOCKS = 16
RG_BLOCK_DIM = D_RGLRU // RG_BLOCKS
CONV_WIDTH = 4
RG_C = 8.0
D_FF = -(-8 * D_MODEL // (3 * 256)) * 256
NORM_EPS = 1e-6
PROJ_SIZES = (MLSTM_HEADS * MLSTM_DK, MLSTM_HEADS * MLSTM_DK, D_MLSTM, D_MLSTM,
              MLSTM_HEADS, MLSTM_HEADS, D_RGLRU, D_RGLRU)
D_IN_PROJ = 2 * MLSTM_HEADS * MLSTM_DK + 2 * D_MLSTM + 2 * MLSTM_HEADS + 2 * D_RGLRU

kernel_name = "hybrid_mlstm_rglru_stream_step"


def rms_norm(x, g):
    xf = x.astype(jnp.float32)
    y = xf * lax.rsqrt(jnp.mean(xf * xf, axis=-1, keepdims=True) + NORM_EPS)
    return (y * g.astype(jnp.float32)).astype(x.dtype)


def mlstm_chunkwise(q, k, v, ig, fg, C0, n0, m0):
    B, H, T, DK = q.shape
    L = min(CHUNK, T)
    nc = T // L
    q = q * (DK ** -0.5)
    logf = jax.nn.log_sigmoid(fg)

    def to_chunks(a):
        return jnp.moveaxis(a.reshape(B, H, nc, L, *a.shape[3:]), 2, 0)

    xs = (to_chunks(q), to_chunks(k), to_chunks(v), to_chunks(ig), to_chunks(logf))
    causal = jnp.tril(jnp.ones((L, L), dtype=bool))

    def step(carry, inp):
        C, n, m = carry
        qc, kc, vc, ic, lfc = inp
        b = jnp.cumsum(lfc, axis=-1)
        logD = b[..., :, None] - b[..., None, :] + ic[..., None, :]
        logD = jnp.where(causal, logD, -jnp.inf)
        inter = b + m[..., None]
        m_j = jnp.maximum(inter, jnp.max(logD, axis=-1))
        Dm = jnp.exp(logD - m_j[..., None])
        w_inter = jnp.exp(inter - m_j)
        s = jnp.einsum('bhjd,bhsd->bhjs', qc, kc) * Dm
        num = (w_inter[..., None] * jnp.einsum('bhjd,bhde->bhje', qc, C)
               + jnp.einsum('bhjs,bhse->bhje', s, vc))
        den = w_inter * jnp.einsum('bhjd,bhd->bhj', qc, n) + jnp.sum(s, axis=-1)
        h = num / jnp.maximum(jnp.abs(den), jnp.exp(-m_j))[..., None]
        m_new = m_j[..., -1]
        w_C = jnp.exp(b[..., -1] + m - m_new)
        w_s = jnp.exp(b[..., -1:] - b + ic - m_new[..., None])
        C_new = w_C[..., None, None] * C + jnp.einsum('bhs,bhsd,bhse->bhde', w_s, kc, vc)
        n_new = w_C[..., None] * n + jnp.einsum('bhs,bhsd->bhd', w_s, kc)
        return (C_new, n_new, m_new), h

    (C, n, m), hs = lax.scan(step, (C0, n0, m0), xs)
    h = jnp.moveaxis(hs, 0, 2).reshape(B, H, T, -1)
    return h, C, n, m


def rglru_branch(xb, conv_state, conv_w, conv_b, w_ga, b_ga, w_gx, b_gx, lam, h0, pos0):
    B, T, DR = xb.shape
    xpad = jnp.concatenate([conv_state.astype(jnp.float32), xb.astype(jnp.float32)], axis=1)
    xc = conv_b.astype(jnp.float32)
    for tap in range(CONV_WIDTH):
        xc = xc + xpad[:, tap:tap + T] * conv_w[tap].astype(jnp.float32)
    new_conv = xpad[:, -(CONV_WIDTH - 1):]
    xblk = xc.reshape(B, T, RG_BLOCKS, RG_BLOCK_DIM)
    r = jax.nn.sigmoid(jnp.einsum('btgi,gij->btgj', xblk, w_ga.astype(jnp.float32)).reshape(B, T, DR)
                       + b_ga.astype(jnp.float32))
    i = jax.nn.sigmoid(jnp.einsum('btgi,gij->btgj', xblk, w_gx.astype(jnp.float32)).reshape(B, T, DR)
                       + b_gx.astype(jnp.float32))
    log_a = -RG_C * r * jax.nn.softplus(-lam.astype(jnp.float32))
    a = jnp.exp(log_a)
    mult = jnp.sqrt(-jnp.expm1(2.0 * log_a))
    reset = ((pos0 + jnp.arange(T)) == 0)[None, :, None]
    mult = jnp.where(reset, 1.0, mult)
    a = jnp.where(reset, 0.0, a)
    bterm = mult * (i * xc)
    bterm = bterm.at[:, 0].add(a[:, 0] * h0.astype(jnp.float32))

    def combine(lhs, rhs):
        return (lhs[0] * rhs[0], rhs[0] * lhs[1] + rhs[1])

    _, h = lax.associative_scan(combine, (a, bterm), axis=1)
    return h, h[:, -1], new_conv


def hybrid_layer(x, C0, n0, m0, h0, conv0, pos0, g_pre_mix, w_in, b_igate, b_fgate, g_mlstm_head,
                 conv_w, conv_b, w_rg_a, b_rg_a, w_rg_x, b_rg_x, rg_lambda, w_out, g_post_mix,
                 g_pre_ffn, w_ffn_gate, w_ffn_up, w_ffn_down, g_post_ffn):
    B, T, _ = x.shape
    u = rms_norm(x, g_pre_mix)
    proj = u @ w_in
    split_idx = np.cumsum(PROJ_SIZES)[:-1].tolist()
    q, k, v, o, ig, fg, xr, gr = jnp.split(proj, split_idx, axis=-1)
    f32 = jnp.float32

    def heads(a, d):
        return a.reshape(B, T, MLSTM_HEADS, d).transpose(0, 2, 1, 3).astype(f32)

    ig_h = (ig.astype(f32) + b_igate.astype(f32)).transpose(0, 2, 1)
    fg_h = (fg.astype(f32) + b_fgate.astype(f32)).transpose(0, 2, 1)
    h_m, C1, n1, m1 = mlstm_chunkwise(heads(q, MLSTM_DK), heads(k, MLSTM_DK), heads(v, MLSTM_DV),
                                      ig_h, fg_h, C0, n0, m0)
    h_m = h_m * lax.rsqrt(jnp.mean(h_m * h_m, axis=-1, keepdims=True) + NORM_EPS)
    h_m = h_m * g_mlstm_head.astype(f32).reshape(MLSTM_HEADS, 1, MLSTM_DV)
    h_m = h_m.transpose(0, 2, 1, 3).reshape(B, T, D_MLSTM)
    out_a = jax.nn.sigmoid(o.astype(f32)) * h_m

    h_r, h1, conv1 = rglru_branch(xr, conv0, conv_w, conv_b, w_rg_a, b_rg_a, w_rg_x, b_rg_x,
                                  rg_lambda, h0, pos0)
    out_b = h_r * jax.nn.gelu(gr.astype(f32), approximate=True)

    mix = jnp.concatenate([out_a, out_b], axis=-1).astype(x.dtype) @ w_out
    x = x + rms_norm(mix, g_post_mix)
    u2 = rms_norm(x, g_pre_ffn)
    ffn = (jax.nn.silu(u2 @ w_ffn_gate) * (u2 @ w_ffn_up)) @ w_ffn_down
    x = x + rms_norm(ffn, g_post_ffn)
    return x, C1, n1, m1, h1, conv1


def setup_inputs(seed: int = 0) -> dict:
    key = jax.random.key(seed)
    ks = jax.random.split(key, 32)
    f32 = jnp.float32

    def nrm(k, shape, scale):
        return jax.random.normal(k, shape, f32) * scale

    def gain(k, shape):
        return 1.0 + 0.01 * jax.random.normal(k, shape, f32)

    u = jax.random.uniform(ks[20], (DEPTH, D_RGLRU), f32, 0.9, 0.999)
    s = u ** (1.0 / RG_C)
    rg_lambda = jnp.log(s) - jnp.log1p(-s)
    return {
        "x_prompt": nrm(ks[0], (BATCH, SEQ, D_MODEL), 1.0),
        "x_sample": nrm(ks[1], (DEC_BATCH, DEC_SEQ, D_MODEL), 1.0),
        "state_mlstm_C": nrm(ks[2], (DEPTH, DEC_BATCH, MLSTM_HEADS, MLSTM_DK, MLSTM_DV), 0.1),
        "state_mlstm_n": nrm(ks[3], (DEPTH, DEC_BATCH, MLSTM_HEADS, MLSTM_DK), 0.5),
        "state_mlstm_m": nrm(ks[4], (DEPTH, DEC_BATCH, MLSTM_HEADS), 0.5),
        "state_rglru_h": nrm(ks[5], (DEPTH, DEC_BATCH, D_RGLRU), 0.5),
        "state_rglru_conv": nrm(ks[6], (DEPTH, DEC_BATCH, CONV_WIDTH - 1, D_RGLRU), 1.0),
        "g_pre_mix": gain(ks[7], (DEPTH, D_MODEL)),
        "w_in": nrm(ks[8], (DEPTH, D_MODEL, D_IN_PROJ), D_MODEL ** -0.5),
        "b_igate": nrm(ks[9], (DEPTH, MLSTM_HEADS), 0.1),
        "b_fgate": 3.0 + nrm(ks[10], (DEPTH, MLSTM_HEADS), 0.5),
        "g_mlstm_head": gain(ks[11], (DEPTH, D_MLSTM)),
        "conv_w": nrm(ks[12], (DEPTH, CONV_WIDTH, D_RGLRU), CONV_WIDTH ** -0.5),
        "conv_b": nrm(ks[13], (DEPTH, D_RGLRU), 0.01),
        "w_rg_a": nrm(ks[14], (DEPTH, RG_BLOCKS, RG_BLOCK_DIM, RG_BLOCK_DIM), RG_BLOCK_DIM ** -0.5),
        "b_rg_a": nrm(ks[15], (DEPTH, D_RGLRU), 0.01),
        "w_rg_x": nrm(ks[16], (DEPTH, RG_BLOCKS, RG_BLOCK_DIM, RG_BLOCK_DIM), RG_BLOCK_DIM ** -0.5),
        "b_rg_x": nrm(ks[17], (DEPTH, D_RGLRU), 0.01),
        "rg_lambda": rg_lambda,
        "w_out": nrm(ks[18], (DEPTH, D_MIX, D_MODEL), D_MIX ** -0.5),
        "g_post_mix": gain(ks[19], (DEPTH, D_MODEL)),
        "g_pre_ffn": gain(ks[21], (DEPTH, D_MODEL)),
        "w_ffn_gate": nrm(ks[22], (DEPTH, D_MODEL, D_FF), D_MODEL ** -0.5),
        "w_ffn_up": nrm(ks[23], (DEPTH, D_MODEL, D_FF), D_MODEL ** -0.5),
        "w_ffn_down": nrm(ks[24], (DEPTH, D_FF, D_MODEL), D_FF ** -0.5),
        "g_post_ffn": gain(ks[25], (DEPTH, D_MODEL)),
    }


def reference(x_prompt, x_sample, state_mlstm_C, state_mlstm_n, state_mlstm_m, state_rglru_h,
              state_rglru_conv, g_pre_mix, w_in, b_igate, b_fgate, g_mlstm_head, conv_w, conv_b,
              w_rg_a, b_rg_a, w_rg_x, b_rg_x, rg_lambda, w_out, g_post_mix, g_pre_ffn,
              w_ffn_gate, w_ffn_up, w_ffn_down, g_post_ffn):
    f32 = jnp.float32
    B = x_prompt.shape[0]
    yp, ys = x_prompt, x_sample
    pC, pn, pm, ph, pconv = [], [], [], [], []
    sC, sn, sm, sh, sconv = [], [], [], [], []
    for l in range(DEPTH):
        w = (g_pre_mix[l], w_in[l], b_igate[l], b_fgate[l], g_mlstm_head[l], conv_w[l], conv_b[l],
             w_rg_a[l], b_rg_a[l], w_rg_x[l], b_rg_x[l], rg_lambda[l], w_out[l], g_post_mix[l],
             g_pre_ffn[l], w_ffn_gate[l], w_ffn_up[l], w_ffn_down[l], g_post_ffn[l])
        C0 = jnp.zeros((B, MLSTM_HEADS, MLSTM_DK, MLSTM_DV), f32)
        n0 = jnp.zeros((B, MLSTM_HEADS, MLSTM_DK), f32)
        m0 = jnp.zeros((B, MLSTM_HEADS), f32)
        h0 = jnp.zeros((B, D_RGLRU), f32)
        cv0 = jnp.zeros((B, CONV_WIDTH - 1, D_RGLRU), f32)
        yp, c1, n1, m1, h1, cv1 = hybrid_layer(yp, C0, n0, m0, h0, cv0, 0, *w)
        pC.append(c1); pn.append(n1); pm.append(m1); ph.append(h1); pconv.append(cv1)
        ys, c2, n2, m2, h2, cv2 = hybrid_layer(
            ys, state_mlstm_C[l].astype(f32), state_mlstm_n[l].astype(f32), state_mlstm_m[l].astype(f32),
            state_rglru_h[l], state_rglru_conv[l], PAST_LEN, *w)
        sC.append(c2); sn.append(n2); sm.append(m2); sh.append(h2); sconv.append(cv2)
    p_C, p_n, p_m, p_h, p_conv = jnp.stack(pC), jnp.stack(pn), jnp.stack(pm), jnp.stack(ph), jnp.stack(pconv)
    s_C, s_n, s_m, s_h, s_conv = jnp.stack(sC), jnp.stack(sn), jnp.stack(sm), jnp.stack(sh), jnp.stack(sconv)
    return (yp, ys, p_C, p_n, p_m, p_h, p_conv, s_C, s_n, s_m, s_h, s_conv)
```

```python
import functools

import jax
import jax.numpy as jnp
from jax import lax
from jax.experimental import pallas as pl
from jax.experimental.pallas import tpu as pltpu

F32 = jnp.float32
BF16 = jnp.bfloat16

NORM_EPS = 1e-6
MLSTM_HEADS = 4
RG_BLOCKS = 16
CONV_WIDTH = 4
RG_C = 8.0
MLSTM_CHUNK = 256

LANES = 128
SUBLANES = 8
VMEM_LIMIT_BYTES = 56 * 1024 * 1024


def _params(semantics):
    return pltpu.CompilerParams(dimension_semantics=semantics, vmem_limit_bytes=VMEM_LIMIT_BYTES)


def _rms(y):
    return y * lax.rsqrt(jnp.mean(y * y, axis=-1, keepdims=True) + NORM_EPS)


def _log_sigmoid(x):
    return jnp.minimum(x, 0.0) - jnp.log1p(jnp.exp(-jnp.abs(x)))


def _softplus(x):
    return jnp.maximum(x, 0.0) + jnp.log1p(jnp.exp(-jnp.abs(x)))


def _sigmoid(x):
    return 1.0 / (1.0 + jnp.exp(-x))


def _gelu_tanh(x):
    return 0.5 * x * (1.0 + jnp.tanh(0.7978845608028654 * (x + 0.044715 * (x * x * x))))


def _norm_gates_kernel(x_ref, g_ref, wg_ref, bias_ref, u_ref, gates_ref, *, heads):
    u = (_rms(x_ref[...]) * g_ref[...]).astype(BF16)
    u_ref[...] = u
    gates = jnp.dot(u, wg_ref[...], preferred_element_type=F32) + bias_ref[...]
    lane = lax.broadcasted_iota(jnp.int32, gates.shape, 1)
    is_forget = (lane >= heads) & (lane < 2 * heads)
    gates_ref[...] = jnp.where(is_forget, _log_sigmoid(gates), gates)


def _norm_gates(x, g, w_gates, gate_bias, *, rows):
    n, d = x.shape
    return pl.pallas_call(
        functools.partial(_norm_gates_kernel, heads=MLSTM_HEADS),
        grid=(n // rows,),
        in_specs=[
            pl.BlockSpec((rows, d), lambda i: (i, 0)),
            pl.BlockSpec((1, d), lambda i: (0, 0)),
            pl.BlockSpec((d, LANES), lambda i: (0, 0)),
            pl.BlockSpec((1, LANES), lambda i: (0, 0)),
        ],
        out_specs=[
            pl.BlockSpec((rows, d), lambda i: (i, 0)),
            pl.BlockSpec((rows, LANES), lambda i: (i, 0)),
        ],
        out_shape=[jax.ShapeDtypeStruct((n, d), BF16), jax.ShapeDtypeStruct((n, LANES), F32)],
        compiler_params=_params(("parallel",)),
        name="norm_gates",
    )(x, g, w_gates, gate_bias)


def _matmul_kernel(a_ref, w_ref, o_ref):
    o_ref[...] = jnp.dot(a_ref[...], w_ref[...], preferred_element_type=F32).astype(o_ref.dtype)


def _matmul(a, w, *, tm, tn, out_dtype):
    m, k = a.shape
    _, n = w.shape
    return pl.pallas_call(
        _matmul_kernel,
        grid=(m // tm, n // tn),
        in_specs=[
            pl.BlockSpec((tm, k), lambda i, j: (i, 0)),
            pl.BlockSpec((k, tn), lambda i, j: (0, j)),
        ],
        out_specs=pl.BlockSpec((tm, tn), lambda i, j: (i, j)),
        out_shape=jax.ShapeDtypeStruct((m, n), out_dtype),
        compiler_params=_params(("parallel", "arbitrary")),
        name="in_proj",
    )(a, w)


def _mlstm_kernel(*refs, heads, dk, dv, has_state):
    if has_state:
        (q_ref, k_ref, v_ref, o_ref, gates_ref, ghead_ref, c0_ref, n0_ref, m0_ref,
         out_ref, c_out, n_out, m_out, c_scr, n_scr, m_scr) = refs
    else:
        (q_ref, k_ref, v_ref, o_ref, gates_ref, ghead_ref,
         out_ref, c_out, n_out, m_out, c_scr, n_scr, m_scr) = refs
    chunk = pl.program_id(1)
    length = q_ref.shape[0]

    @pl.when(chunk == 0)
    def _init():
        if has_state:
            c_scr[...] = c0_ref[...]
            n_scr[...] = n0_ref[...]
            m_scr[...] = m0_ref[...]
        else:
            c_scr[...] = jnp.zeros_like(c_scr)
            n_scr[...] = jnp.zeros_like(n_scr)
            m_scr[...] = jnp.zeros_like(m_scr)

    row = lax.broadcasted_iota(jnp.int32, (length, length), 0)
    col = lax.broadcasted_iota(jnp.int32, (length, length), 1)
    causal = col <= row
    eye = col == row
    gates = gates_ref[...]

    for h in range(heads):
        ig_col = gates[:, h:h + 1]
        lf_col = gates[:, heads + h:heads + h + 1]
        b_row = jnp.sum(jnp.where(row <= col, lf_col, 0.0), axis=0, keepdims=True)
        b_col = jnp.sum(jnp.where(eye, b_row, 0.0), axis=1, keepdims=True)
        ig_row = jnp.sum(jnp.where(eye, ig_col, 0.0), axis=0, keepdims=True)
        log_d = jnp.where(causal, (b_col - b_row) + ig_row, -jnp.inf)
        m_prev = m_scr[h:h + 1, 0:1]
        inter = b_col + m_prev
        m_tok = jnp.maximum(inter, jnp.max(log_d, axis=1, keepdims=True))
        d_mat = jnp.exp(log_d - m_tok)
        w_inter = jnp.exp(inter - m_tok)

        q = q_ref[:, h * dk:(h + 1) * dk] * (dk ** -0.5)
        k = k_ref[:, h * dk:(h + 1) * dk]
        qb = q.astype(BF16)
        vb = v_ref[:, h * dv:(h + 1) * dv].astype(BF16)
        s = lax.dot_general(qb, k.astype(BF16), (((1,), (1,)), ((), ())),
                            preferred_element_type=F32) * d_mat
        c_old = c_scr[h]
        n_old = n_scr[h:h + 1, :]
        num = (w_inter * jnp.dot(qb, c_old.astype(BF16), preferred_element_type=F32)
               + jnp.dot(s.astype(BF16), vb, preferred_element_type=F32))
        den = (w_inter * jnp.sum(q * n_old, axis=1, keepdims=True)
               + jnp.sum(s, axis=1, keepdims=True))
        hid = num / jnp.maximum(jnp.abs(den), jnp.exp(-m_tok))

        m_new = m_tok[length - 1:length, :]
        b_last = b_col[length - 1:length, :]
        w_c = jnp.exp(b_last + m_prev - m_new)
        w_s = jnp.exp((b_last - b_col) + ig_col - m_new)
        kw = k * w_s
        c_scr[h] = w_c * c_old + lax.dot_general(
            kw.astype(BF16), vb, (((0,), (0,)), ((), ())), preferred_element_type=F32)
        n_scr[h:h + 1, :] = w_c * n_old + jnp.sum(kw, axis=0, keepdims=True)
        m_scr[h:h + 1, :] = jnp.broadcast_to(m_new, (1, m_scr.shape[1]))

        normed = _rms(hid) * ghead_ref[:, h * dv:(h + 1) * dv]
        out_ref[:, h * dv:(h + 1) * dv] = (
            _sigmoid(o_ref[:, h * dv:(h + 1) * dv]) * normed).astype(out_ref.dtype)

    @pl.when(chunk == pl.num_programs(1) - 1)
    def _emit_state():
        c_out[...] = c_scr[...]
        n_out[...] = n_scr[...]
        m_out[...] = m_scr[...]


def _mlstm(proj, gates, g_head, state, *, batch, seq):
    heads = MLSTM_HEADS
    d_mlstm = g_head.shape[-1]
    dv = d_mlstm // heads
    dk = dv // 2
    length = min(MLSTM_CHUNK, seq)
    nc = seq // length
    has_state = state is not None

    def rows(b, c):
        return b * nc + c

    in_specs = [
        pl.BlockSpec((length, heads * dk), lambda b, c: (rows(b, c), 0)),
        pl.BlockSpec((length, heads * dk), lambda b, c: (rows(b, c), 1)),
        pl.BlockSpec((length, d_mlstm), lambda b, c: (rows(b, c), 1)),
        pl.BlockSpec((length, d_mlstm), lambda b, c: (rows(b, c), 2)),
        pl.BlockSpec((length, LANES), lambda b, c: (rows(b, c), 0)),
        pl.BlockSpec((1, d_mlstm), lambda b, c: (0, 0)),
    ]
    args = [proj, proj, proj, proj, gates, g_head]
    if has_state:
        c0, n0, m0 = state
        in_specs += [
            pl.BlockSpec((None, heads, dk, dv), lambda b, c: (b, 0, 0, 0)),
            pl.BlockSpec((None, heads, dk), lambda b, c: (b, 0, 0)),
            pl.BlockSpec((None, heads, LANES), lambda b, c: (b, 0, 0)),
        ]
        args += [c0, n0, jnp.broadcast_to(m0[..., None], (batch, heads, LANES))]
    out, c1, n1, m1 = pl.pallas_call(
        functools.partial(_mlstm_kernel, heads=heads, dk=dk, dv=dv, has_state=has_state),
        grid=(batch, nc),
        in_specs=in_specs,
        out_specs=[
            pl.BlockSpec((length, d_mlstm), lambda b, c: (rows(b, c), 0)),
            pl.BlockSpec((None, heads, dk, dv), lambda b, c: (b, 0, 0, 0)),
            pl.BlockSpec((None, heads, dk), lambda b, c: (b, 0, 0)),
            pl.BlockSpec((None, heads, LANES), lambda b, c: (b, 0, 0)),
        ],
        out_shape=[
            jax.ShapeDtypeStruct((batch * seq, d_mlstm), BF16),
            jax.ShapeDtypeStruct((batch, heads, dk, dv), F32),
            jax.ShapeDtypeStruct((batch, heads, dk), F32),
            jax.ShapeDtypeStruct((batch, heads, LANES), F32),
        ],
        scratch_shapes=[
            pltpu.VMEM((heads, dk, dv), F32),
            pltpu.VMEM((heads, dk), F32),
            pltpu.VMEM((heads, LANES), F32),
        ],
        compiler_params=_params(("parallel", "arbitrary")),
        name="mlstm",
    )(*args)
    return out, c1, n1, m1[..., 0]


def _rglru_kernel(*refs, reset_first, has_state):
    if has_state:
        (x_ref, gr_ref, convw_ref, convb_ref, wcat_ref, bga_ref, bgx_ref, lam_ref, tail0_ref, h0_ref,
         out_ref, hlast_ref, tail_out_ref, tail_scr, h_scr) = refs
    else:
        (x_ref, gr_ref, convw_ref, convb_ref, wcat_ref, bga_ref, bgx_ref, lam_ref,
         out_ref, hlast_ref, tail_out_ref, tail_scr, h_scr) = refs
    t = pl.program_id(2)
    tt, cb = x_ref.shape

    @pl.when(t == 0)
    def _init():
        if has_state:
            tail_scr[...] = tail0_ref[...]
            h_scr[...] = h0_ref[...]
        else:
            tail_scr[...] = jnp.zeros_like(tail_scr)
            h_scr[...] = jnp.zeros_like(h_scr)

    row8 = lax.broadcasted_iota(jnp.int32, (SUBLANES, LANES), 0)
    row = lax.broadcasted_iota(jnp.int32, (tt, LANES), 0)
    row_in_group = row & (SUBLANES - 1)
    first_token = (row == 0) & (t == 0)

    for blk in range(cb // LANES):
        lanes = slice(blk * LANES, (blk + 1) * LANES)
        x = x_ref[:, lanes]
        tail = tail_scr[:, lanes]
        xc = convb_ref[:, lanes]
        for tap in range(CONV_WIDTH - 1):
            back = CONV_WIDTH - 1 - tap
            shifted = pltpu.roll(x, back, 0)
            head = jnp.where(row8 < back, pltpu.roll(tail, back, 0), shifted[0:SUBLANES])
            shifted = jnp.concatenate([head, shifted[SUBLANES:]], axis=0)
            xc = xc + shifted * convw_ref[tap:tap + 1, lanes]
        xc = xc + x * convw_ref[CONV_WIDTH - 1:CONV_WIDTH, lanes]
        tail_scr[:, lanes] = x[tt - SUBLANES:tt]

        pre = jnp.dot(xc.astype(BF16), wcat_ref[blk], preferred_element_type=F32)
        r_gate = _sigmoid(pre[:, :LANES] + bga_ref[:, lanes])
        i_gate = _sigmoid(pre[:, LANES:] + bgx_ref[:, lanes])
        log_a = -RG_C * r_gate * _softplus(-lam_ref[:, lanes])
        a = jnp.exp(log_a)
        mult = jnp.sqrt(1.0 - jnp.exp(2.0 * log_a))
        if reset_first:
            mult = jnp.where(first_token, 1.0, mult)
            a = jnp.where(first_token, 0.0, a)
        b = mult * (i_gate * xc)

        for step in (1, 2, 4):
            valid = row_in_group >= step
            a_prev = pltpu.roll(a, step, 0)
            b_prev = pltpu.roll(b, step, 0)
            b = jnp.where(valid, a * b_prev + b, b)
            a = jnp.where(valid, a * a_prev, a)
        carry = h_scr[:, lanes]
        hs = []
        for grp in range(tt // SUBLANES):
            rows = slice(grp * SUBLANES, (grp + 1) * SUBLANES)
            h_grp = a[rows] * carry + b[rows]
            hs.append(h_grp)
            carry = h_grp[SUBLANES - 1:SUBLANES, :]
        h_scr[:, lanes] = carry
        h_all = jnp.concatenate(hs, axis=0)
        out_ref[:, lanes] = (h_all * _gelu_tanh(gr_ref[:, lanes])).astype(out_ref.dtype)

    @pl.when(t == pl.num_programs(2) - 1)
    def _emit_state():
        hlast_ref[...] = jnp.broadcast_to(h_scr[...], hlast_ref.shape)
        tail_out_ref[...] = tail_scr[...]


def _rglru(proj, rg, state, *, batch, seq, reset_first):
    conv_w, conv_b, w_cat, b_ga, b_gx, lam = rg
    d_rg = conv_b.shape[-1]
    cb = 4 * LANES
    tt = min(256, seq)
    nt = seq // tt
    ncb = d_rg // cb
    xr_off = 6144 // cb
    gr_off = 8192 // cb
    has_state = state is not None

    def vec(rows):
        return pl.BlockSpec((rows, cb), lambda b, j, t: (0, j))

    in_specs = [
        pl.BlockSpec((tt, cb), lambda b, j, t: (b * nt + t, xr_off + j)),
        pl.BlockSpec((tt, cb), lambda b, j, t: (b * nt + t, gr_off + j)),
        vec(CONV_WIDTH), vec(1),
        pl.BlockSpec((cb // LANES, LANES, 2 * LANES), lambda b, j, t: (j, 0, 0)),
        vec(1), vec(1), vec(1),
    ]
    args = [proj, proj, conv_w, conv_b, w_cat, b_ga, b_gx, lam]
    if has_state:
        h0, conv0 = state
        tail0 = jnp.pad(conv0, ((0, 0), (SUBLANES - (CONV_WIDTH - 1), 0), (0, 0)))
        in_specs += [
            pl.BlockSpec((None, SUBLANES, cb), lambda b, j, t: (b, 0, j)),
            pl.BlockSpec((None, 1, cb), lambda b, j, t: (b, 0, j)),
        ]
        args += [tail0, h0[:, None, :]]
    out, h_last, tail = pl.pallas_call(
        functools.partial(_rglru_kernel, reset_first=reset_first, has_state=has_state),
        grid=(batch, ncb, nt),
        in_specs=in_specs,
        out_specs=[
            pl.BlockSpec((tt, cb), lambda b, j, t: (b * nt + t, j)),
            pl.BlockSpec((None, SUBLANES, cb), lambda b, j, t: (b, 0, j)),
            pl.BlockSpec((None, SUBLANES, cb), lambda b, j, t: (b, 0, j)),
        ],
        out_shape=[
            jax.ShapeDtypeStruct((batch * seq, d_rg), BF16),
            jax.ShapeDtypeStruct((batch, SUBLANES, d_rg), F32),
            jax.ShapeDtypeStruct((batch, SUBLANES, d_rg), F32),
        ],
        scratch_shapes=[pltpu.VMEM((SUBLANES, cb), F32), pltpu.VMEM((1, cb), F32)],
        compiler_params=_params(("parallel", "parallel", "arbitrary")),
        name="rglru",
    )(*args)
    return out, h_last[:, 0, :], tail[:, SUBLANES - (CONV_WIDTH - 1):, :]


def _norm_rows(src_ref, fn, *, chunk):
    n = src_ref.shape[0] // chunk

    def body(i, carry):
        rows = pl.ds(pl.multiple_of(i * chunk, chunk), chunk)
        fn(rows, src_ref[rows, :])
        return carry

    lax.fori_loop(0, n, body, 0)


def _out_proj_kernel(a_ref, b_ref, w_ref, res_ref, g1_ref, g2_ref, x1_ref, u2_ref, *, nka):
    k = pl.program_id(1)

    @pl.when(k == 0)
    def _zero():
        x1_ref[...] = jnp.zeros_like(x1_ref)

    @pl.when(k < nka)
    def _from_a():
        x1_ref[...] += jnp.dot(a_ref[...], w_ref[...], preferred_element_type=F32)

    @pl.when(k >= nka)
    def _from_b():
        x1_ref[...] += jnp.dot(b_ref[...], w_ref[...], preferred_element_type=F32)

    @pl.when(k == pl.num_programs(1) - 1)
    def _epilogue():
        def fn(rows, y):
            x1 = res_ref[rows, :] + _rms(y) * g1_ref[...]
            x1_ref[rows, :] = x1
            u2_ref[rows, :] = (_rms(x1) * g2_ref[...]).astype(u2_ref.dtype)

        _norm_rows(x1_ref, fn, chunk=64)


def _out_proj(a, b, w, res, g1, g2, *, tm, tk):
    m, ka = a.shape
    kb = b.shape[1]
    n = w.shape[1]
    nka = ka // tk
    nk = nka + kb // tk
    return pl.pallas_call(
        functools.partial(_out_proj_kernel, nka=nka),
        grid=(m // tm, nk),
        in_specs=[
            pl.BlockSpec((tm, tk), lambda i, k: (i, jnp.minimum(k, nka - 1))),
            pl.BlockSpec((tm, tk), lambda i, k: (i, jnp.maximum(k - nka, 0))),
            pl.BlockSpec((tk, n), lambda i, k: (k, 0)),
            pl.BlockSpec((tm, n), lambda i, k: (i, 0)),
            pl.BlockSpec((1, n), lambda i, k: (0, 0)),
            pl.BlockSpec((1, n), lambda i, k: (0, 0)),
        ],
        out_specs=[
            pl.BlockSpec((tm, n), lambda i, k: (i, 0)),
            pl.BlockSpec((tm, n), lambda i, k: (i, 0)),
        ],
        out_shape=[jax.ShapeDtypeStruct((m, n), F32), jax.ShapeDtypeStruct((m, n), BF16)],
        compiler_params=_params(("parallel", "arbitrary")),
        name="out_proj",
    )(a, b, w, res, g1, g2)


def _ffn_kernel(u_ref, wg_ref, wu_ref, wd_ref, res_ref, g_ref, o_ref):
    j = pl.program_id(1)

    @pl.when(j == 0)
    def _zero():
        o_ref[...] = jnp.zeros_like(o_ref)

    u = u_ref[...]
    gate = jnp.dot(u, wg_ref[...], preferred_element_type=F32)
    up = jnp.dot(u, wu_ref[...], preferred_element_type=F32)
    hidden = (gate * _sigmoid(gate) * up).astype(BF16)
    o_ref[...] += jnp.dot(hidden, wd_ref[...], preferred_element_type=F32)

    @pl.when(j == pl.num_programs(1) - 1)
    def _epilogue():
        def fn(rows, y):
            o_ref[rows, :] = res_ref[rows, :] + _rms(y) * g_ref[...]

        _norm_rows(o_ref, fn, chunk=64)


def _ffn(u, wg, wu, wd, res, g, *, tm, tf):
    m, d = u.shape
    f = wg.shape[1]
    return pl.pallas_call(
        _ffn_kernel,
        grid=(m // tm, f // tf),
        in_specs=[
            pl.BlockSpec((tm, d), lambda i, j: (i, 0)),
            pl.BlockSpec((d, tf), lambda i, j: (0, j)),
            pl.BlockSpec((d, tf), lambda i, j: (0, j)),
            pl.BlockSpec((tf, d), lambda i, j: (j, 0)),
            pl.BlockSpec((tm, d), lambda i, j: (i, 0)),
            pl.BlockSpec((1, d), lambda i, j: (0, 0)),
        ],
        out_specs=pl.BlockSpec((tm, d), lambda i, j: (i, 0)),
        out_shape=jax.ShapeDtypeStruct((m, d), F32),
        compiler_params=_params(("parallel", "arbitrary")),
        name="ffn",
    )(u, wg, wu, wd, res, g)


def _prepare_weights(g_pre_mix, w_in, b_igate, b_fgate, g_mlstm_head, conv_w, conv_b, w_rg_a, b_rg_a,
                     w_rg_x, b_rg_x, rg_lambda, w_out, g_post_mix, g_pre_ffn, w_ffn_gate, w_ffn_up,
                     w_ffn_down, g_post_ffn):
    heads = MLSTM_HEADS
    d_mlstm = g_mlstm_head.shape[-1]
    n_dense = heads * (d_mlstm // heads // 2) * 2 + 2 * d_mlstm
    gate_lo, gate_hi = n_dense, n_dense + 2 * heads
    w_main = jnp.concatenate([w_in[:, :gate_lo], w_in[:, gate_hi:]], axis=1).astype(BF16)
    w_gates = jnp.pad(w_in[:, gate_lo:gate_hi], ((0, 0), (0, LANES - 2 * heads))).astype(BF16)
    gate_bias = jnp.pad(jnp.concatenate([b_igate, b_fgate]), (0, LANES - 2 * heads))[None, :]
    row = lambda v: v[None, :].astype(F32)
    rg = (conv_w.astype(F32), row(conv_b),
          jnp.concatenate([w_rg_a, w_rg_x], axis=-1).astype(BF16),
          row(b_rg_a), row(b_rg_x), row(rg_lambda))
    return dict(
        g_pre_mix=row(g_pre_mix), w_main=w_main, w_gates=w_gates, gate_bias=gate_bias.astype(F32),
        g_head=row(g_mlstm_head), rg=rg, w_out=w_out.astype(BF16), g_post_mix=row(g_post_mix),
        g_pre_ffn=row(g_pre_ffn), w_gate=w_ffn_gate.astype(BF16), w_up=w_ffn_up.astype(BF16),
        w_down=w_ffn_down.astype(BF16), g_post_ffn=row(g_post_ffn))


def _layer(x, mlstm_state, rglru_state, w, *, reset_first):
    batch, seq, d = x.shape
    n = batch * seq
    x2 = x.reshape(n, d)
    tm = min(n, 1024)
    u, gates = _norm_gates(x2, w["g_pre_mix"], w["w_gates"], w["gate_bias"], rows=256)
    proj = _matmul(u, w["w_main"], tm=tm, tn=1024, out_dtype=F32)
    out_a, c1, n1, m1 = _mlstm(proj, gates, w["g_head"], mlstm_state, batch=batch, seq=seq)
    out_b, h1, conv1 = _rglru(proj, w["rg"], rglru_state, batch=batch, seq=seq, reset_first=reset_first)
    x1, u2 = _out_proj(out_a, out_b, w["w_out"], x2, w["g_post_mix"], w["g_pre_ffn"], tm=256, tk=512)
    y = _ffn(u2, w["w_gate"], w["w_up"], w["w_down"], x1, w["g_post_ffn"], tm=512, tf=256)
    return y.reshape(batch, seq, d), c1, n1, m1, h1, conv1


def kernel(x_prompt, x_sample, state_mlstm_C, state_mlstm_n, state_mlstm_m, state_rglru_h, state_rglru_conv, g_pre_mix, w_in, b_igate, b_fgate, g_mlstm_head, conv_w, conv_b, w_rg_a, b_rg_a, w_rg_x, b_rg_x, rg_lambda, w_out, g_post_mix, g_pre_ffn, w_ffn_gate, w_ffn_up, w_ffn_down, g_post_ffn):
    depth = w_in.shape[0]
    yp, ys = x_prompt, x_sample
    prompt_states, sample_states = [], []
    for l in range(depth):
        w = _prepare_weights(
            g_pre_mix[l], w_in[l], b_igate[l], b_fgate[l], g_mlstm_head[l], conv_w[l], conv_b[l],
            w_rg_a[l], b_rg_a[l], w_rg_x[l], b_rg_x[l], rg_lambda[l], w_out[l], g_post_mix[l],
            g_pre_ffn[l], w_ffn_gate[l], w_ffn_up[l], w_ffn_down[l], g_post_ffn[l])
        yp, *st = _layer(yp, None, None, w, reset_first=True)
        prompt_states.append(st)
        ys, *st = _layer(
            ys, (state_mlstm_C[l].astype(F32), state_mlstm_n[l].astype(F32), state_mlstm_m[l].astype(F32)),
            (state_rglru_h[l].astype(F32), state_rglru_conv[l].astype(F32)), w, reset_first=False)
        sample_states.append(st)
    stack = lambda states, i: jnp.stack([s[i] for s in states])
    return (yp, ys,
            *(stack(prompt_states, i) for i in range(5)),
            *(stack(sample_states, i) for i in range(5)))
```

```python
import functools

import jax
import jax.numpy as jnp
from jax import lax
from jax.experimental import pallas as pl
from jax.experimental.pallas import tpu as pltpu

F32 = jnp.float32
BF16 = jnp.bfloat16

NORM_EPS = 1e-6
MLSTM_HEADS = 4
RG_BLOCKS = 16
CONV_WIDTH = 4
RG_C = 8.0
MLSTM_CHUNK = 256

LANES = 128
SUBLANES = 8
VMEM_LIMIT_BYTES = 56 * 1024 * 1024

NORM_ROWS = 256
IN_PROJ_TM = 1024
IN_PROJ_TN = 1024
ROW_RESIDENT_TM = 512
OUT_PROJ_TK = 512
FFN_TF = 256


def _params(semantics):
    return pltpu.CompilerParams(dimension_semantics=semantics, vmem_limit_bytes=VMEM_LIMIT_BYTES)


def _rms(y):
    return y * lax.rsqrt(jnp.mean(y * y, axis=-1, keepdims=True) + NORM_EPS)


def _log_sigmoid(x):
    return jnp.minimum(x, 0.0) - jnp.log1p(jnp.exp(-jnp.abs(x)))


def _softplus(x):
    return jnp.maximum(x, 0.0) + jnp.log1p(jnp.exp(-jnp.abs(x)))


def _sigmoid(x):
    return 1.0 / (1.0 + jnp.exp(-x))


def _gelu_tanh(x):
    return 0.5 * x * (1.0 + jnp.tanh(0.7978845608028654 * (x + 0.044715 * (x * x * x))))


def _norm_gates_kernel(x_ref, g_ref, wg_ref, bias_ref, u_ref, gates_ref, *, heads):
    u = (_rms(x_ref[...]) * g_ref[...]).astype(BF16)
    u_ref[...] = u
    gates = jnp.dot(u, wg_ref[...], preferred_element_type=F32) + bias_ref[...]
    lane = lax.broadcasted_iota(jnp.int32, gates.shape, 1)
    is_forget = (lane >= heads) & (lane < 2 * heads)
    gates_ref[...] = jnp.where(is_forget, _log_sigmoid(gates), gates)


def _norm_gates(x, g, w_gates, gate_bias, *, rows):
    n, d = x.shape
    return pl.pallas_call(
        functools.partial(_norm_gates_kernel, heads=MLSTM_HEADS),
        grid=(n // rows,),
        in_specs=[
            pl.BlockSpec((rows, d), lambda i: (i, 0)),
            pl.BlockSpec((1, d), lambda i: (0, 0)),
            pl.BlockSpec((d, LANES), lambda i: (0, 0)),
            pl.BlockSpec((1, LANES), lambda i: (0, 0)),
        ],
        out_specs=[
            pl.BlockSpec((rows, d), lambda i: (i, 0)),
            pl.BlockSpec((rows, LANES), lambda i: (i, 0)),
        ],
        out_shape=[jax.ShapeDtypeStruct((n, d), BF16), jax.ShapeDtypeStruct((n, LANES), F32)],
        compiler_params=_params(("parallel",)),
        name="norm_gates",
    )(x, g, w_gates, gate_bias)


def _matmul_kernel(a_ref, w_ref, o_ref):
    o_ref[...] = jnp.dot(a_ref[...], w_ref[...], preferred_element_type=F32).astype(o_ref.dtype)


def _matmul(a, w, *, tm, out_dtype):
    m, k = a.shape
    nblk, _, tn = w.shape
    return pl.pallas_call(
        _matmul_kernel,
        grid=(m // tm, nblk),
        in_specs=[
            pl.BlockSpec((tm, k), lambda i, j: (i, 0)),
            pl.BlockSpec((None, k, tn), lambda i, j: (j, 0, 0)),
        ],
        out_specs=pl.BlockSpec((tm, tn), lambda i, j: (i, j)),
        out_shape=jax.ShapeDtypeStruct((m, nblk * tn), out_dtype),
        compiler_params=_params(("parallel", "arbitrary")),
        name="in_proj",
    )(a, w)


def _mlstm_kernel(*refs, heads, dk, dv, has_state):
    if has_state:
        (q_ref, k_ref, v_ref, o_ref, gates_ref, ghead_ref, c0_ref, n0_ref, m0_ref,
         out_ref, c_out, n_out, m_out, c_scr, n_scr, m_scr) = refs
    else:
        (q_ref, k_ref, v_ref, o_ref, gates_ref, ghead_ref,
         out_ref, c_out, n_out, m_out, c_scr, n_scr, m_scr) = refs
    chunk = pl.program_id(1)
    length = q_ref.shape[0]

    @pl.when(chunk == 0)
    def _init():
        if has_state:
            c_scr[...] = c0_ref[...]
            n_scr[...] = n0_ref[...]
            m_scr[...] = m0_ref[...]
        else:
            c_scr[...] = jnp.zeros_like(c_scr)
            n_scr[...] = jnp.zeros_like(n_scr)
            m_scr[...] = jnp.zeros_like(m_scr)

    row = lax.broadcasted_iota(jnp.int32, (length, length), 0)
    col = lax.broadcasted_iota(jnp.int32, (length, length), 1)
    causal = col <= row
    eye = col == row
    gates = gates_ref[...]

    for h in range(heads):
        ig_col = gates[:, h:h + 1]
        lf_col = gates[:, heads + h:heads + h + 1]
        b_row = jnp.sum(jnp.where(row <= col, lf_col, 0.0), axis=0, keepdims=True)
        b_col = jnp.sum(jnp.where(eye, b_row, 0.0), axis=1, keepdims=True)
        ig_row = jnp.sum(jnp.where(eye, ig_col, 0.0), axis=0, keepdims=True)
        log_d = jnp.where(causal, (b_col - b_row) + ig_row, -jnp.inf)
        m_prev = m_scr[h:h + 1, 0:1]
        inter = b_col + m_prev
        m_tok = jnp.maximum(inter, jnp.max(log_d, axis=1, keepdims=True))
        d_mat = jnp.exp(log_d - m_tok)
        w_inter = jnp.exp(inter - m_tok)

        q = q_ref[:, h * dk:(h + 1) * dk] * (dk ** -0.5)
        k = k_ref[:, h * dk:(h + 1) * dk]
        qb = q.astype(BF16)
        vb = v_ref[:, h * dv:(h + 1) * dv].astype(BF16)
        s = lax.dot_general(qb, k.astype(BF16), (((1,), (1,)), ((), ())),
                            preferred_element_type=F32) * d_mat
        c_old = c_scr[h]
        n_old = n_scr[h:h + 1, :]
        num = (w_inter * jnp.dot(qb, c_old.astype(BF16), preferred_element_type=F32)
               + jnp.dot(s.astype(BF16), vb, preferred_element_type=F32))
        den = (w_inter * jnp.sum(q * n_old, axis=1, keepdims=True)
               + jnp.sum(s, axis=1, keepdims=True))
        hid = num / jnp.maximum(jnp.abs(den), jnp.exp(-m_tok))

        m_new = m_tok[length - 1:length, :]
        b_last = b_col[length - 1:length, :]
        w_c = jnp.exp(b_last + m_prev - m_new)
        w_s = jnp.exp((b_last - b_col) + ig_col - m_new)
        kw = k * w_s
        c_scr[h] = w_c * c_old + lax.dot_general(
            kw.astype(BF16), vb, (((0,), (0,)), ((), ())), preferred_element_type=F32)
        n_scr[h:h + 1, :] = w_c * n_old + jnp.sum(kw, axis=0, keepdims=True)
        m_scr[h:h + 1, :] = jnp.broadcast_to(m_new, (1, m_scr.shape[1]))

        normed = _rms(hid) * ghead_ref[:, h * dv:(h + 1) * dv]
        out_ref[:, h * dv:(h + 1) * dv] = (
            _sigmoid(o_ref[:, h * dv:(h + 1) * dv]) * normed).astype(out_ref.dtype)

    @pl.when(chunk == pl.num_programs(1) - 1)
    def _emit_state():
        c_out[...] = c_scr[...]
        n_out[...] = n_scr[...]
        m_out[...] = m_scr[...]


def _mlstm(proj, gates, g_head, state, *, batch, seq):
    heads = MLSTM_HEADS
    d_mlstm = g_head.shape[-1]
    dv = d_mlstm // heads
    dk = dv // 2
    length = min(MLSTM_CHUNK, seq)
    nc = seq // length
    has_state = state is not None

    def rows(b, c):
        return b * nc + c

    in_specs = [
        pl.BlockSpec((length, heads * dk), lambda b, c: (rows(b, c), 0)),
        pl.BlockSpec((length, heads * dk), lambda b, c: (rows(b, c), 1)),
        pl.BlockSpec((length, d_mlstm), lambda b, c: (rows(b, c), 1)),
        pl.BlockSpec((length, d_mlstm), lambda b, c: (rows(b, c), 2)),
        pl.BlockSpec((length, LANES), lambda b, c: (rows(b, c), 0)),
        pl.BlockSpec((1, d_mlstm), lambda b, c: (0, 0)),
    ]
    args = [proj, proj, proj, proj, gates, g_head]
    if has_state:
        c0, n0, m0 = state
        in_specs += [
            pl.BlockSpec((None, heads, dk, dv), lambda b, c: (b, 0, 0, 0)),
            pl.BlockSpec((None, heads, dk), lambda b, c: (b, 0, 0)),
            pl.BlockSpec((None, heads, LANES), lambda b, c: (b, 0, 0)),
        ]
        args += [c0, n0, jnp.broadcast_to(m0[..., None], (batch, heads, LANES))]
    out, c1, n1, m1 = pl.pallas_call(
        functools.partial(_mlstm_kernel, heads=heads, dk=dk, dv=dv, has_state=has_state),
        grid=(batch, nc),
        in_specs=in_specs,
        out_specs=[
            pl.BlockSpec((length, d_mlstm), lambda b, c: (rows(b, c), 0)),
            pl.BlockSpec((None, heads, dk, dv), lambda b, c: (b, 0, 0, 0)),
            pl.BlockSpec((None, heads, dk), lambda b, c: (b, 0, 0)),
            pl.BlockSpec((None, heads, LANES), lambda b, c: (b, 0, 0)),
        ],
        out_shape=[
            jax.ShapeDtypeStruct((batch * seq, d_mlstm), BF16),
            jax.ShapeDtypeStruct((batch, heads, dk, dv), F32),
            jax.ShapeDtypeStruct((batch, heads, dk), F32),
            jax.ShapeDtypeStruct((batch, heads, LANES), F32),
        ],
        scratch_shapes=[
            pltpu.VMEM((heads, dk, dv), F32),
            pltpu.VMEM((heads, dk), F32),
            pltpu.VMEM((heads, LANES), F32),
        ],
        compiler_params=_params(("parallel", "arbitrary")),
        name="mlstm",
    )(*args)
    return out, c1, n1, m1[..., 0]


def _rglru_kernel(*refs, reset_first, has_state):
    if has_state:
        (x_ref, gr_ref, convw_ref, convb_ref, wcat_ref, bga_ref, bgx_ref, lam_ref, tail0_ref, h0_ref,
         out_ref, hlast_ref, tail_out_ref, tail_scr, h_scr) = refs
    else:
        (x_ref, gr_ref, convw_ref, convb_ref, wcat_ref, bga_ref, bgx_ref, lam_ref,
         out_ref, hlast_ref, tail_out_ref, tail_scr, h_scr) = refs
    t = pl.program_id(2)
    tt, cb = x_ref.shape

    @pl.when(t == 0)
    def _init():
        if has_state:
            tail_scr[...] = tail0_ref[...]
            h_scr[...] = h0_ref[...]
        else:
            tail_scr[...] = jnp.zeros_like(tail_scr)
            h_scr[...] = jnp.zeros_like(h_scr)

    row8 = lax.broadcasted_iota(jnp.int32, (SUBLANES, LANES), 0)
    row = lax.broadcasted_iota(jnp.int32, (tt, LANES), 0)
    row_in_group = row & (SUBLANES - 1)
    first_token = (row == 0) & (t == 0)

    for blk in range(cb // LANES):
        lanes = slice(blk * LANES, (blk + 1) * LANES)
        x = x_ref[:, lanes]
        tail = tail_scr[:, lanes]
        xc = convb_ref[:, lanes]
        for tap in range(CONV_WIDTH - 1):
            back = CONV_WIDTH - 1 - tap
            shifted = pltpu.roll(x, back, 0)
            head = jnp.where(row8 < back, pltpu.roll(tail, back, 0), shifted[0:SUBLANES])
            shifted = jnp.concatenate([head, shifted[SUBLANES:]], axis=0)
            xc = xc + shifted * convw_ref[tap:tap + 1, lanes]
        xc = xc + x * convw_ref[CONV_WIDTH - 1:CONV_WIDTH, lanes]
        tail_scr[:, lanes] = x[tt - SUBLANES:tt]

        pre = jnp.dot(xc.astype(BF16), wcat_ref[blk], preferred_element_type=F32)
        r_gate = _sigmoid(pre[:, :LANES] + bga_ref[:, lanes])
        i_gate = _sigmoid(pre[:, LANES:] + bgx_ref[:, lanes])
        log_a = -RG_C * r_gate * _softplus(-lam_ref[:, lanes])
        a = jnp.exp(log_a)
        mult = jnp.sqrt(1.0 - jnp.exp(2.0 * log_a))
        if reset_first:
            mult = jnp.where(first_token, 1.0, mult)
            a = jnp.where(first_token, 0.0, a)
        b = mult * (i_gate * xc)

        for step in (1, 2, 4):
            valid = row_in_group >= step
            a_prev = pltpu.roll(a, step, 0)
            b_prev = pltpu.roll(b, step, 0)
            b = jnp.where(valid, a * b_prev + b, b)
            a = jnp.where(valid, a * a_prev, a)
        carry = h_scr[:, lanes]
        hs = []
        for grp in range(tt // SUBLANES):
            rows = slice(grp * SUBLANES, (grp + 1) * SUBLANES)
            h_grp = a[rows] * carry + b[rows]
            hs.append(h_grp)
            carry = h_grp[SUBLANES - 1:SUBLANES, :]
        h_scr[:, lanes] = carry
        h_all = jnp.concatenate(hs, axis=0)
        out_ref[:, lanes] = (h_all * _gelu_tanh(gr_ref[:, lanes])).astype(out_ref.dtype)

    @pl.when(t == pl.num_programs(2) - 1)
    def _emit_state():
        hlast_ref[...] = jnp.broadcast_to(h_scr[...], hlast_ref.shape)
        tail_out_ref[...] = tail_scr[...]


def _rglru(proj, rg, state, *, batch, seq, reset_first, xr_col, gr_col):
    conv_w, conv_b, w_cat, b_ga, b_gx, lam = rg
    d_rg = conv_b.shape[-1]
    cb = 4 * LANES
    tt = min(256, seq)
    nt = seq // tt
    ncb = d_rg // cb
    xr_off = xr_col // cb
    gr_off = gr_col // cb
    has_state = state is not None

    def vec(rows):
        return pl.BlockSpec((rows, cb), lambda b, j, t: (0, j))

    in_specs = [
        pl.BlockSpec((tt, cb), lambda b, j, t: (b * nt + t, xr_off + j)),
        pl.BlockSpec((tt, cb), lambda b, j, t: (b * nt + t, gr_off + j)),
        vec(CONV_WIDTH), vec(1),
        pl.BlockSpec((cb // LANES, LANES, 2 * LANES), lambda b, j, t: (j, 0, 0)),
        vec(1), vec(1), vec(1),
    ]
    args = [proj, proj, conv_w, conv_b, w_cat, b_ga, b_gx, lam]
    if has_state:
        h0, conv0 = state
        tail0 = jnp.pad(conv0, ((0, 0), (SUBLANES - (CONV_WIDTH - 1), 0), (0, 0)))
        in_specs += [
            pl.BlockSpec((None, SUBLANES, cb), lambda b, j, t: (b, 0, j)),
            pl.BlockSpec((None, 1, cb), lambda b, j, t: (b, 0, j)),
        ]
        args += [tail0, h0[:, None, :]]
    out, h_last, tail = pl.pallas_call(
        functools.partial(_rglru_kernel, reset_first=reset_first, has_state=has_state),
        grid=(batch, ncb, nt),
        in_specs=in_specs,
        out_specs=[
            pl.BlockSpec((tt, cb), lambda b, j, t: (b * nt + t, j)),
            pl.BlockSpec((None, SUBLANES, cb), lambda b, j, t: (b, 0, j)),
            pl.BlockSpec((None, SUBLANES, cb), lambda b, j, t: (b, 0, j)),
        ],
        out_shape=[
            jax.ShapeDtypeStruct((batch * seq, d_rg), BF16),
            jax.ShapeDtypeStruct((batch, SUBLANES, d_rg), F32),
            jax.ShapeDtypeStruct((batch, SUBLANES, d_rg), F32),
        ],
        scratch_shapes=[pltpu.VMEM((SUBLANES, cb), F32), pltpu.VMEM((1, cb), F32)],
        compiler_params=_params(("parallel", "parallel", "arbitrary")),
        name="rglru",
    )(*args)
    return out, h_last[:, 0, :], tail[:, SUBLANES - (CONV_WIDTH - 1):, :]


def _norm_rows(src_ref, fn, *, chunk):
    n = src_ref.shape[0] // chunk

    def body(i, carry):
        rows = pl.ds(pl.multiple_of(i * chunk, chunk), chunk)
        fn(rows, src_ref[rows, :])
        return carry

    lax.fori_loop(0, n, body, 0)


def _out_proj_kernel(a_ref, b_ref, w_ref, res_ref, g1_ref, g2_ref, x1_ref, u2_ref, *, nka):
    k = pl.program_id(1)

    @pl.when(k == 0)
    def _zero():
        x1_ref[...] = jnp.zeros_like(x1_ref)

    @pl.when(k < nka)
    def _from_a():
        x1_ref[...] += jnp.dot(a_ref[...], w_ref[...], preferred_element_type=F32)

    @pl.when(k >= nka)
    def _from_b():
        x1_ref[...] += jnp.dot(b_ref[...], w_ref[...], preferred_element_type=F32)

    @pl.when(k == pl.num_programs(1) - 1)
    def _epilogue():
        def fn(rows, y):
            x1 = res_ref[rows, :] + _rms(y) * g1_ref[...]
            x1_ref[rows, :] = x1
            u2_ref[rows, :] = (_rms(x1) * g2_ref[...]).astype(u2_ref.dtype)

        _norm_rows(x1_ref, fn, chunk=64)


def _out_proj(a, b, w, res, g1, g2, *, tm, tk):
    m, ka = a.shape
    kb = b.shape[1]
    n = w.shape[1]
    nka = ka // tk
    nk = nka + kb // tk
    return pl.pallas_call(
        functools.partial(_out_proj_kernel, nka=nka),
        grid=(m // tm, nk),
        in_specs=[
            pl.BlockSpec((tm, tk), lambda i, k: (i, jnp.minimum(k, nka - 1))),
            pl.BlockSpec((tm, tk), lambda i, k: (i, jnp.maximum(k - nka, 0))),
            pl.BlockSpec((tk, n), lambda i, k: (k, 0)),
            pl.BlockSpec((tm, n), lambda i, k: (i, 0)),
            pl.BlockSpec((1, n), lambda i, k: (0, 0)),
            pl.BlockSpec((1, n), lambda i, k: (0, 0)),
        ],
        out_specs=[
            pl.BlockSpec((tm, n), lambda i, k: (i, 0)),
            pl.BlockSpec((tm, n), lambda i, k: (i, 0)),
        ],
        out_shape=[jax.ShapeDtypeStruct((m, n), F32), jax.ShapeDtypeStruct((m, n), BF16)],
        compiler_params=_params(("parallel", "arbitrary")),
        name="out_proj",
    )(a, b, w, res, g1, g2)


def _ffn_kernel(u_ref, wgu_ref, wd_ref, res_ref, g_ref, o_ref):
    j = pl.program_id(1)
    tf = wd_ref.shape[0]

    @pl.when(j == 0)
    def _zero():
        o_ref[...] = jnp.zeros_like(o_ref)

    gate_up = jnp.dot(u_ref[...], wgu_ref[...], preferred_element_type=F32)
    gate = gate_up[:, :tf]
    up = gate_up[:, tf:]
    hidden = (gate * _sigmoid(gate) * up).astype(BF16)
    o_ref[...] += jnp.dot(hidden, wd_ref[...], preferred_element_type=F32)

    @pl.when(j == pl.num_programs(1) - 1)
    def _epilogue():
        def fn(rows, y):
            o_ref[rows, :] = res_ref[rows, :] + _rms(y) * g_ref[...]

        _norm_rows(o_ref, fn, chunk=64)


def _ffn(u, wgu, wd, res, g, *, tm):
    m, d = u.shape
    nblk, _, tf2 = wgu.shape
    tf = tf2 // 2
    return pl.pallas_call(
        _ffn_kernel,
        grid=(m // tm, nblk),
        in_specs=[
            pl.BlockSpec((tm, d), lambda i, j: (i, 0)),
            pl.BlockSpec((None, d, tf2), lambda i, j: (j, 0, 0)),
            pl.BlockSpec((tf, d), lambda i, j: (j, 0)),
            pl.BlockSpec((tm, d), lambda i, j: (i, 0)),
            pl.BlockSpec((1, d), lambda i, j: (0, 0)),
        ],
        out_specs=pl.BlockSpec((tm, d), lambda i, j: (i, 0)),
        out_shape=jax.ShapeDtypeStruct((m, d), F32),
        compiler_params=_params(("parallel", "arbitrary")),
        name="ffn",
    )(u, wgu, wd, res, g)


def _prepare_weights(g_pre_mix, w_in, b_igate, b_fgate, g_mlstm_head, conv_w, conv_b, w_rg_a, b_rg_a,
                     w_rg_x, b_rg_x, rg_lambda, w_out, g_post_mix, g_pre_ffn, w_ffn_gate, w_ffn_up,
                     w_ffn_down, g_post_ffn):
    heads = MLSTM_HEADS
    d_mlstm = g_mlstm_head.shape[-1]
    n_dense = heads * (d_mlstm // heads // 2) * 2 + 2 * d_mlstm
    gate_lo, gate_hi = n_dense, n_dense + 2 * heads
    d = w_in.shape[0]

    def col_tiles(cols, tn):
        return cols.reshape(d, cols.shape[1] // tn, tn).transpose(1, 0, 2)

    w_main = jnp.concatenate(
        [col_tiles(w_in[:, :gate_lo], IN_PROJ_TN), col_tiles(w_in[:, gate_hi:], IN_PROJ_TN)],
        axis=0).astype(BF16)
    w_gate_up = jnp.concatenate(
        [col_tiles(w_ffn_gate, FFN_TF), col_tiles(w_ffn_up, FFN_TF)], axis=2).astype(BF16)
    w_gates = jnp.pad(w_in[:, gate_lo:gate_hi], ((0, 0), (0, LANES - 2 * heads))).astype(BF16)
    gate_bias = jnp.pad(jnp.concatenate([b_igate, b_fgate]), (0, LANES - 2 * heads))[None, :]
    row = lambda v: v[None, :].astype(F32)
    rg = (conv_w.astype(F32), row(conv_b),
          jnp.concatenate([w_rg_a, w_rg_x], axis=-1).astype(BF16),
          row(b_rg_a), row(b_rg_x), row(rg_lambda))
    return dict(
        g_pre_mix=row(g_pre_mix), w_main=w_main, w_gates=w_gates, gate_bias=gate_bias.astype(F32),
        g_head=row(g_mlstm_head), rg=rg, w_out=w_out.astype(BF16), g_post_mix=row(g_post_mix),
        g_pre_ffn=row(g_pre_ffn), w_gate_up=w_gate_up, w_down=w_ffn_down.astype(BF16),
        g_post_ffn=row(g_post_ffn))


def _layer(x, mlstm_state, rglru_state, w, *, reset_first):
    batch, seq, d = x.shape
    n = batch * seq
    x2 = x.reshape(n, d)
    d_mlstm = w["g_head"].shape[-1]
    n_dense = 3 * d_mlstm
    d_rg = w["rg"][1].shape[-1]
    u, gates = _norm_gates(x2, w["g_pre_mix"], w["w_gates"], w["gate_bias"], rows=NORM_ROWS)
    proj = _matmul(u, w["w_main"], tm=min(n, IN_PROJ_TM), out_dtype=F32)
    out_a, c1, n1, m1 = _mlstm(proj, gates, w["g_head"], mlstm_state, batch=batch, seq=seq)
    out_b, h1, conv1 = _rglru(proj, w["rg"], rglru_state, batch=batch, seq=seq, reset_first=reset_first,
                              xr_col=n_dense, gr_col=n_dense + d_rg)
    x1, u2 = _out_proj(out_a, out_b, w["w_out"], x2, w["g_post_mix"], w["g_pre_ffn"],
                       tm=min(n, ROW_RESIDENT_TM), tk=OUT_PROJ_TK)
    y = _ffn(u2, w["w_gate_up"], w["w_down"], x1, w["g_post_ffn"], tm=min(n, ROW_RESIDENT_TM))
    return y.reshape(batch, seq, d), c1, n1, m1, h1, conv1


def kernel(x_prompt, x_sample, state_mlstm_C, state_mlstm_n, state_mlstm_m, state_rglru_h, state_rglru_conv, g_pre_mix, w_in, b_igate, b_fgate, g_mlstm_head, conv_w, conv_b, w_rg_a, b_rg_a, w_rg_x, b_rg_x, rg_lambda, w_out, g_post_mix, g_pre_ffn, w_ffn_gate, w_ffn_up, w_ffn_down, g_post_ffn):
    depth = w_in.shape[0]
    yp, ys = x_prompt, x_sample
    prompt_states, sample_states = [], []
    for l in range(depth):
        w = _prepare_weights(
            g_pre_mix[l], w_in[l], b_igate[l], b_fgate[l], g_mlstm_head[l], conv_w[l], conv_b[l],
            w_rg_a[l], b_rg_a[l], w_rg_x[l], b_rg_x[l], rg_lambda[l], w_out[l], g_post_mix[l],
            g_pre_ffn[l], w_ffn_gate[l], w_ffn_up[l], w_ffn_down[l], g_post_ffn[l])
        yp, *st = _layer(yp, None, None, w, reset_first=True)
        prompt_states.append(st)
        ys, *st = _layer(
            ys, (state_mlstm_C[l].astype(F32), state_mlstm_n[l].astype(F32), state_mlstm_m[l].astype(F32)),
            (state_rglru_h[l].astype(F32), state_rglru_conv[l].astype(F32)), w, reset_first=False)
        sample_states.append(st)
    stack = lambda states, i: jnp.stack([s[i] for s in states])
    return (yp, ys,
            *(stack(prompt_states, i) for i in range(5)),
            *(stack(sample_states, i) for i in range(5)))
```

```python
import functools

import jax
import jax.numpy as jnp
from jax import lax
from jax.experimental import pallas as pl
from jax.experimental.pallas import tpu as pltpu

F32 = jnp.float32
BF16 = jnp.bfloat16

NORM_EPS = 1e-6
MLSTM_HEADS = 4
RG_BLOCKS = 16
CONV_WIDTH = 4
RG_C = 8.0
MLSTM_CHUNK = 256

LANES = 128
SUBLANES = 8
VMEM_LIMIT_BYTES = 56 * 1024 * 1024

NORM_ROWS = 256
IN_PROJ_TM = 1024
IN_PROJ_TN = 1024
ROW_RESIDENT_TM = 512
OUT_PROJ_TK = 512
FFN_TF = 256
RGLRU_TT = 128
RGLRU_CB = 1024


def _params(semantics):
    return pltpu.CompilerParams(dimension_semantics=semantics, vmem_limit_bytes=VMEM_LIMIT_BYTES)


def _rms(y):
    return y * lax.rsqrt(jnp.mean(y * y, axis=-1, keepdims=True) + NORM_EPS)


def _log_sigmoid(x):
    return jnp.minimum(x, 0.0) - jnp.log1p(jnp.exp(-jnp.abs(x)))


def _softplus(x):
    return jnp.maximum(x, 0.0) + jnp.log1p(jnp.exp(-jnp.abs(x)))


def _sigmoid(x):
    return 1.0 / (1.0 + jnp.exp(-x))


def _sigmoid_tanh(x):
    return 0.5 * jnp.tanh(0.5 * x) + 0.5


def _gelu_tanh(x):
    return 0.5 * x * (1.0 + jnp.tanh(0.7978845608028654 * (x + 0.044715 * (x * x * x))))


def _norm_gates_kernel(x_ref, g_ref, wg_ref, bias_ref, u_ref, gates_ref, *, heads):
    u = (_rms(x_ref[...]) * g_ref[...]).astype(BF16)
    u_ref[...] = u
    gates = jnp.dot(u, wg_ref[...], preferred_element_type=F32) + bias_ref[...]
    lane = lax.broadcasted_iota(jnp.int32, gates.shape, 1)
    is_forget = (lane >= heads) & (lane < 2 * heads)
    gates_ref[...] = jnp.where(is_forget, _log_sigmoid(gates), gates)


def _norm_gates(x, g, w_gates, gate_bias, *, rows):
    n, d = x.shape
    return pl.pallas_call(
        functools.partial(_norm_gates_kernel, heads=MLSTM_HEADS),
        grid=(n // rows,),
        in_specs=[
            pl.BlockSpec((rows, d), lambda i: (i, 0)),
            pl.BlockSpec((1, d), lambda i: (0, 0)),
            pl.BlockSpec((d, LANES), lambda i: (0, 0)),
            pl.BlockSpec((1, LANES), lambda i: (0, 0)),
        ],
        out_specs=[
            pl.BlockSpec((rows, d), lambda i: (i, 0)),
            pl.BlockSpec((rows, LANES), lambda i: (i, 0)),
        ],
        out_shape=[jax.ShapeDtypeStruct((n, d), BF16), jax.ShapeDtypeStruct((n, LANES), F32)],
        compiler_params=_params(("parallel",)),
        name="norm_gates",
    )(x, g, w_gates, gate_bias)


def _matmul_kernel(a_ref, w_ref, o_ref):
    o_ref[...] = jnp.dot(a_ref[...], w_ref[...], preferred_element_type=F32).astype(o_ref.dtype)


def _matmul(a, w, *, tm, tn, out_dtype, name):
    m, k = a.shape
    n = w.shape[1]
    return pl.pallas_call(
        _matmul_kernel,
        grid=(m // tm, n // tn),
        in_specs=[
            pl.BlockSpec((tm, k), lambda i, j: (i, 0)),
            pl.BlockSpec((k, tn), lambda i, j: (0, j)),
        ],
        out_specs=pl.BlockSpec((tm, tn), lambda i, j: (i, j)),
        out_shape=jax.ShapeDtypeStruct((m, n), out_dtype),
        compiler_params=_params(("parallel", "arbitrary")),
        name=name,
    )(a, w)


def _mlstm_kernel(*refs, heads, dk, dv, has_state):
    if has_state:
        (q_ref, k_ref, v_ref, o_ref, gates_ref, ghead_ref, c0_ref, n0_ref, m0_ref,
         out_ref, c_out, n_out, m_out, c_scr, n_scr, m_scr) = refs
    else:
        (q_ref, k_ref, v_ref, o_ref, gates_ref, ghead_ref,
         out_ref, c_out, n_out, m_out, c_scr, n_scr, m_scr) = refs
    chunk = pl.program_id(1)
    length = q_ref.shape[0]

    @pl.when(chunk == 0)
    def _init():
        if has_state:
            c_scr[...] = c0_ref[...]
            n_scr[...] = n0_ref[...]
            m_scr[...] = m0_ref[...]
        else:
            c_scr[...] = jnp.zeros_like(c_scr)
            n_scr[...] = jnp.zeros_like(n_scr)
            m_scr[...] = jnp.zeros_like(m_scr)

    row = lax.broadcasted_iota(jnp.int32, (length, length), 0)
    col = lax.broadcasted_iota(jnp.int32, (length, length), 1)
    causal = col <= row
    eye = col == row
    gates = gates_ref[...]

    for h in range(heads):
        ig_col = gates[:, h:h + 1]
        lf_col = gates[:, heads + h:heads + h + 1]
        b_row = jnp.sum(jnp.where(row <= col, lf_col, 0.0), axis=0, keepdims=True)
        b_col = jnp.sum(jnp.where(eye, b_row, 0.0), axis=1, keepdims=True)
        ig_row = jnp.sum(jnp.where(eye, ig_col, 0.0), axis=0, keepdims=True)
        log_d = jnp.where(causal, (b_col - b_row) + ig_row, -jnp.inf)
        m_prev = m_scr[h:h + 1, 0:1]
        inter = b_col + m_prev
        m_tok = jnp.maximum(inter, jnp.max(log_d, axis=1, keepdims=True))
        d_mat = jnp.exp(log_d - m_tok)
        w_inter = jnp.exp(inter - m_tok)

        q = q_ref[:, h * dk:(h + 1) * dk] * (dk ** -0.5)
        k = k_ref[:, h * dk:(h + 1) * dk]
        qb = q.astype(BF16)
        vb = v_ref[:, h * dv:(h + 1) * dv].astype(BF16)
        s = lax.dot_general(qb, k.astype(BF16), (((1,), (1,)), ((), ())),
                            preferred_element_type=F32) * d_mat
        c_old = c_scr[h]
        n_old = n_scr[h:h + 1, :]
        num = (w_inter * jnp.dot(qb, c_old.astype(BF16), preferred_element_type=F32)
               + jnp.dot(s.astype(BF16), vb, preferred_element_type=F32))
        den = (w_inter * jnp.sum(q * n_old, axis=1, keepdims=True)
               + jnp.sum(s, axis=1, keepdims=True))
        hid = num / jnp.maximum(jnp.abs(den), jnp.exp(-m_tok))

        m_new = m_tok[length - 1:length, :]
        b_last = b_col[length - 1:length, :]
        w_c = jnp.exp(b_last + m_prev - m_new)
        w_s = jnp.exp((b_last - b_col) + ig_col - m_new)
        kw = k * w_s
        c_scr[h] = w_c * c_old + lax.dot_general(
            kw.astype(BF16), vb, (((0,), (0,)), ((), ())), preferred_element_type=F32)
        n_scr[h:h + 1, :] = w_c * n_old + jnp.sum(kw, axis=0, keepdims=True)
        m_scr[h:h + 1, :] = jnp.broadcast_to(m_new, (1, m_scr.shape[1]))

        normed = _rms(hid) * ghead_ref[:, h * dv:(h + 1) * dv]
        out_ref[:, h * dv:(h + 1) * dv] = (
            _sigmoid(o_ref[:, h * dv:(h + 1) * dv]) * normed).astype(out_ref.dtype)

    @pl.when(chunk == pl.num_programs(1) - 1)
    def _emit_state():
        c_out[...] = c_scr[...]
        n_out[...] = n_scr[...]
        m_out[...] = m_scr[...]


def _mlstm(proj, gates, g_head, state, *, batch, seq):
    heads = MLSTM_HEADS
    d_mlstm = g_head.shape[-1]
    dv = d_mlstm // heads
    dk = dv // 2
    length = min(MLSTM_CHUNK, seq)
    nc = seq // length
    has_state = state is not None

    def rows(b, c):
        return b * nc + c

    in_specs = [
        pl.BlockSpec((length, heads * dk), lambda b, c: (rows(b, c), 0)),
        pl.BlockSpec((length, heads * dk), lambda b, c: (rows(b, c), 1)),
        pl.BlockSpec((length, d_mlstm), lambda b, c: (rows(b, c), 1)),
        pl.BlockSpec((length, d_mlstm), lambda b, c: (rows(b, c), 2)),
        pl.BlockSpec((length, LANES), lambda b, c: (rows(b, c), 0)),
        pl.BlockSpec((1, d_mlstm), lambda b, c: (0, 0)),
    ]
    args = [proj, proj, proj, proj, gates, g_head]
    if has_state:
        c0, n0, m0 = state
        in_specs += [
            pl.BlockSpec((None, heads, dk, dv), lambda b, c: (b, 0, 0, 0)),
            pl.BlockSpec((None, heads, dk), lambda b, c: (b, 0, 0)),
            pl.BlockSpec((None, heads, LANES), lambda b, c: (b, 0, 0)),
        ]
        args += [c0, n0, jnp.broadcast_to(m0[..., None], (batch, heads, LANES))]
    out, c1, n1, m1 = pl.pallas_call(
        functools.partial(_mlstm_kernel, heads=heads, dk=dk, dv=dv, has_state=has_state),
        grid=(batch, nc),
        in_specs=in_specs,
        out_specs=[
            pl.BlockSpec((length, d_mlstm), lambda b, c: (rows(b, c), 0)),
            pl.BlockSpec((None, heads, dk, dv), lambda b, c: (b, 0, 0, 0)),
            pl.BlockSpec((None, heads, dk), lambda b, c: (b, 0, 0)),
            pl.BlockSpec((None, heads, LANES), lambda b, c: (b, 0, 0)),
        ],
        out_shape=[
            jax.ShapeDtypeStruct((batch * seq, d_mlstm), BF16),
            jax.ShapeDtypeStruct((batch, heads, dk, dv), F32),
            jax.ShapeDtypeStruct((batch, heads, dk), F32),
            jax.ShapeDtypeStruct((batch, heads, LANES), F32),
        ],
        scratch_shapes=[
            pltpu.VMEM((heads, dk, dv), F32),
            pltpu.VMEM((heads, dk), F32),
            pltpu.VMEM((heads, LANES), F32),
        ],
        compiler_params=_params(("parallel", "arbitrary")),
        name="mlstm",
    )(*args)
    return out, c1, n1, m1[..., 0]


def _rglru_kernel(*refs, reset_first, has_state):
    if has_state:
        (x_ref, gr_ref, convw_ref, convb_ref, wcat_ref, bga_ref, bgx_ref, lam_ref, tail0_ref, h0_ref,
         out_ref, hlast_ref, tail_out_ref, xs_scr, a_scr, b_scr, h_scr) = refs
    else:
        (x_ref, gr_ref, convw_ref, convb_ref, wcat_ref, bga_ref, bgx_ref, lam_ref,
         out_ref, hlast_ref, tail_out_ref, xs_scr, a_scr, b_scr, h_scr) = refs
    t_idx = pl.program_id(2)
    nb, tt, cb = x_ref.shape
    rows = nb * tt
    halo = (CONV_WIDTH - 1) * nb

    nblk = cb // LANES
    blocks = [slice(blk * LANES, (blk + 1) * LANES) for blk in range(nblk)]

    @pl.when(t_idx == 0)
    def _init():
        for blk, lanes in enumerate(blocks):
            if has_state:
                xs_scr[blk, 0:halo, :] = tail0_ref[:, lanes]
                h_scr[blk] = h0_ref[:, lanes]
            else:
                xs_scr[blk, 0:halo, :] = jnp.zeros((halo, LANES), F32)
                h_scr[blk] = jnp.zeros((nb, LANES), F32)

    @pl.when(t_idx > 0)
    def _carry_history():
        xs_scr[:, 0:halo, :] = xs_scr[:, rows:rows + halo, :]

    row = lax.broadcasted_iota(jnp.int32, (rows, LANES), 0)
    first_token = (row < nb) & (t_idx == 0)
    for blk, lanes in enumerate(blocks):
        for s in range(nb):
            xs_scr[blk, pl.ds(halo + s, tt, stride=nb), :] = x_ref[s, :, lanes]
        xc = convb_ref[:, lanes]
        for tap in range(CONV_WIDTH):
            xc = xc + xs_scr[blk, tap * nb:tap * nb + rows, :] * convw_ref[tap:tap + 1, lanes]
        pre = jnp.dot(xc.astype(BF16), wcat_ref[blk], preferred_element_type=F32)
        r_gate = _sigmoid_tanh(pre[:, :LANES] + bga_ref[:, lanes])
        i_gate = _sigmoid_tanh(pre[:, LANES:] + bgx_ref[:, lanes])
        log_a = -RG_C * r_gate * _softplus(-lam_ref[:, lanes])
        a = jnp.exp(log_a)
        gap = 1.0 - a * a
        mult = jnp.where(gap > 0.0, gap * lax.rsqrt(gap), 0.0)
        if reset_first:
            mult = jnp.where(first_token, 1.0, mult)
            a = jnp.where(first_token, 0.0, a)
        a_scr[blk] = a
        b_scr[blk] = mult * (i_gate * xc)

    def scan_step(s, h):
        rows_s = pl.ds(pl.multiple_of(s * nb, nb), nb)
        h = a_scr[:, rows_s, :] * h + b_scr[:, rows_s, :]
        b_scr[:, rows_s, :] = h
        return h

    h_scr[...] = lax.fori_loop(0, tt, scan_step, h_scr[...], unroll=8)

    for blk, lanes in enumerate(blocks):
        for s in range(nb):
            h_stream = b_scr[blk, pl.ds(s, tt, stride=nb), :]
            out_ref[s, :, lanes] = (h_stream * _gelu_tanh(gr_ref[s, :, lanes])).astype(out_ref.dtype)

    @pl.when(t_idx == pl.num_programs(2) - 1)
    def _emit_state():
        for blk, lanes in enumerate(blocks):
            hlast_ref[:, lanes] = h_scr[blk]
            tail_out_ref[:, lanes] = xs_scr[blk, rows:rows + halo, :]


def _rglru(proj3, rg, state, *, reset_first, xr_col, gr_col):
    conv_w, conv_b, w_cat, b_ga, b_gx, lam = rg
    batch, seq, _ = proj3.shape
    d_rg = conv_b.shape[-1]
    nb = SUBLANES
    groups = batch // nb
    cb = RGLRU_CB
    tt = min(RGLRU_TT, seq)
    nt = seq // tt
    ncb = d_rg // cb
    xr_off = xr_col // cb
    gr_off = gr_col // cb
    halo = (CONV_WIDTH - 1) * nb
    has_state = state is not None

    def vec(rows):
        return pl.BlockSpec((rows, cb), lambda g, j, t: (0, j))

    in_specs = [
        pl.BlockSpec((nb, tt, cb), lambda g, j, t: (g, t, xr_off + j)),
        pl.BlockSpec((nb, tt, cb), lambda g, j, t: (g, t, gr_off + j)),
        vec(CONV_WIDTH), vec(1),
        pl.BlockSpec((cb // LANES, LANES, 2 * LANES), lambda g, j, t: (j, 0, 0)),
        vec(1), vec(1), vec(1),
    ]
    args = [proj3, proj3, conv_w, conv_b, w_cat, b_ga, b_gx, lam]
    if has_state:
        h0, conv0 = state
        tail0 = conv0.reshape(groups, nb, CONV_WIDTH - 1, d_rg).transpose(0, 2, 1, 3).reshape(groups, halo, d_rg)
        in_specs += [
            pl.BlockSpec((None, halo, cb), lambda g, j, t: (g, 0, j)),
            pl.BlockSpec((nb, cb), lambda g, j, t: (g, j)),
        ]
        args += [tail0, h0]
    out, h_last, tail = pl.pallas_call(
        functools.partial(_rglru_kernel, reset_first=reset_first, has_state=has_state),
        grid=(groups, ncb, nt),
        in_specs=in_specs,
        out_specs=[
            pl.BlockSpec((nb, tt, cb), lambda g, j, t: (g, t, j)),
            pl.BlockSpec((nb, cb), lambda g, j, t: (g, j)),
            pl.BlockSpec((None, halo, cb), lambda g, j, t: (g, 0, j)),
        ],
        out_shape=[
            jax.ShapeDtypeStruct((batch, seq, d_rg), BF16),
            jax.ShapeDtypeStruct((batch, d_rg), F32),
            jax.ShapeDtypeStruct((groups, halo, d_rg), F32),
        ],
        scratch_shapes=[
            pltpu.VMEM((cb // LANES, nb * tt + halo, LANES), F32),
            pltpu.VMEM((cb // LANES, nb * tt, LANES), F32),
            pltpu.VMEM((cb // LANES, nb * tt, LANES), F32),
            pltpu.VMEM((cb // LANES, nb, LANES), F32),
        ],
        compiler_params=_params(("parallel", "parallel", "arbitrary")),
        name="rglru",
    )(*args)
    conv1 = tail.reshape(groups, CONV_WIDTH - 1, nb, d_rg).transpose(0, 2, 1, 3).reshape(
        batch, CONV_WIDTH - 1, d_rg)
    return out, h_last, conv1


def _norm_rows(src_ref, fn, *, chunk):
    n = src_ref.shape[0] // chunk

    def body(i, carry):
        rows = pl.ds(pl.multiple_of(i * chunk, chunk), chunk)
        fn(rows, src_ref[rows, :])
        return carry

    lax.fori_loop(0, n, body, 0)


def _out_proj_kernel(a_ref, b_ref, w_ref, res_ref, g1_ref, g2_ref, x1_ref, u2_ref, *, nka):
    k = pl.program_id(1)

    @pl.when(k == 0)
    def _zero():
        x1_ref[...] = jnp.zeros_like(x1_ref)

    @pl.when(k < nka)
    def _from_a():
        x1_ref[...] += jnp.dot(a_ref[...], w_ref[...], preferred_element_type=F32)

    @pl.when(k >= nka)
    def _from_b():
        x1_ref[...] += jnp.dot(b_ref[...], w_ref[...], preferred_element_type=F32)

    @pl.when(k == pl.num_programs(1) - 1)
    def _epilogue():
        def fn(rows, y):
            x1 = res_ref[rows, :] + _rms(y) * g1_ref[...]
            x1_ref[rows, :] = x1
            u2_ref[rows, :] = (_rms(x1) * g2_ref[...]).astype(u2_ref.dtype)

        _norm_rows(x1_ref, fn, chunk=64)


def _out_proj(a, b, w, res, g1, g2, *, tm, tk):
    m, ka = a.shape
    kb = b.shape[1]
    n = w.shape[1]
    nka = ka // tk
    nk = nka + kb // tk
    return pl.pallas_call(
        functools.partial(_out_proj_kernel, nka=nka),
        grid=(m // tm, nk),
        in_specs=[
            pl.BlockSpec((tm, tk), lambda i, k: (i, jnp.minimum(k, nka - 1))),
            pl.BlockSpec((tm, tk), lambda i, k: (i, jnp.maximum(k - nka, 0))),
            pl.BlockSpec((tk, n), lambda i, k: (k, 0)),
            pl.BlockSpec((tm, n), lambda i, k: (i, 0)),
            pl.BlockSpec((1, n), lambda i, k: (0, 0)),
            pl.BlockSpec((1, n), lambda i, k: (0, 0)),
        ],
        out_specs=[
            pl.BlockSpec((tm, n), lambda i, k: (i, 0)),
            pl.BlockSpec((tm, n), lambda i, k: (i, 0)),
        ],
        out_shape=[jax.ShapeDtypeStruct((m, n), F32), jax.ShapeDtypeStruct((m, n), BF16)],
        compiler_params=_params(("parallel", "arbitrary")),
        name="out_proj",
    )(a, b, w, res, g1, g2)


def _ffn_kernel(u_ref, wg_ref, wu_ref, wd_ref, res_ref, g_ref, o_ref):
    j = pl.program_id(1)

    @pl.when(j == 0)
    def _zero():
        o_ref[...] = jnp.zeros_like(o_ref)

    u = u_ref[...]
    gate = jnp.dot(u, wg_ref[...], preferred_element_type=F32)
    up = jnp.dot(u, wu_ref[...], preferred_element_type=F32)
    hidden = (gate * _sigmoid(gate) * up).astype(BF16)
    o_ref[...] += jnp.dot(hidden, wd_ref[...], preferred_element_type=F32)

    @pl.when(j == pl.num_programs(1) - 1)
    def _epilogue():
        def fn(rows, y):
            o_ref[rows, :] = res_ref[rows, :] + _rms(y) * g_ref[...]

        _norm_rows(o_ref, fn, chunk=64)


def _ffn(u, wg, wu, wd, res, g, *, tm, tf):
    m, d = u.shape
    f = wg.shape[1]
    return pl.pallas_call(
        _ffn_kernel,
        grid=(m // tm, f // tf),
        in_specs=[
            pl.BlockSpec((tm, d), lambda i, j: (i, 0)),
            pl.BlockSpec((d, tf), lambda i, j: (0, j)),
            pl.BlockSpec((d, tf), lambda i, j: (0, j)),
            pl.BlockSpec((tf, d), lambda i, j: (j, 0)),
            pl.BlockSpec((tm, d), lambda i, j: (i, 0)),
            pl.BlockSpec((1, d), lambda i, j: (0, 0)),
        ],
        out_specs=pl.BlockSpec((tm, d), lambda i, j: (i, 0)),
        out_shape=jax.ShapeDtypeStruct((m, d), F32),
        compiler_params=_params(("parallel", "arbitrary")),
        name="ffn",
    )(u, wg, wu, wd, res, g)


def _prepare_weights(g_pre_mix, w_in, b_igate, b_fgate, g_mlstm_head, conv_w, conv_b, w_rg_a, b_rg_a,
                     w_rg_x, b_rg_x, rg_lambda, w_out, g_post_mix, g_pre_ffn, w_ffn_gate, w_ffn_up,
                     w_ffn_down, g_post_ffn):
    heads = MLSTM_HEADS
    d_mlstm = g_mlstm_head.shape[-1]
    n_dense = heads * (d_mlstm // heads // 2) * 2 + 2 * d_mlstm
    gate_lo, gate_hi = n_dense, n_dense + 2 * heads
    w_dense = w_in[:, :gate_lo].astype(BF16)
    w_rg = w_in[:, gate_hi:].astype(BF16)
    w_gates = jnp.pad(w_in[:, gate_lo:gate_hi], ((0, 0), (0, LANES - 2 * heads))).astype(BF16)
    gate_bias = jnp.pad(jnp.concatenate([b_igate, b_fgate]), (0, LANES - 2 * heads))[None, :]
    row = lambda v: v[None, :].astype(F32)
    rg = (conv_w.astype(F32), row(conv_b),
          jnp.concatenate([w_rg_a, w_rg_x], axis=-1).astype(BF16),
          row(b_rg_a), row(b_rg_x), row(rg_lambda))
    return dict(
        g_pre_mix=row(g_pre_mix), w_dense=w_dense, w_rg=w_rg, w_gates=w_gates,
        gate_bias=gate_bias.astype(F32), g_head=row(g_mlstm_head), rg=rg, w_out=w_out.astype(BF16),
        g_post_mix=row(g_post_mix), g_pre_ffn=row(g_pre_ffn), w_gate=w_ffn_gate.astype(BF16),
        w_up=w_ffn_up.astype(BF16), w_down=w_ffn_down.astype(BF16), g_post_ffn=row(g_post_ffn))


def _layer(x, mlstm_state, rglru_state, w, *, reset_first):
    batch, seq, d = x.shape
    n = batch * seq
    x2 = x.reshape(n, d)
    d_rg = w["rg"][1].shape[-1]
    tm = min(n, IN_PROJ_TM)
    u, gates = _norm_gates(x2, w["g_pre_mix"], w["w_gates"], w["gate_bias"], rows=NORM_ROWS)
    proj_dense = _matmul(u, w["w_dense"], tm=tm, tn=IN_PROJ_TN, out_dtype=F32, name="in_proj_dense")
    proj_rg = _matmul(u, w["w_rg"], tm=tm, tn=IN_PROJ_TN, out_dtype=F32, name="in_proj_rg")
    out_a, c1, n1, m1 = _mlstm(proj_dense, gates, w["g_head"], mlstm_state, batch=batch, seq=seq)
    out_b, h1, conv1 = _rglru(proj_rg.reshape(batch, seq, 2 * d_rg), w["rg"], rglru_state,
                              reset_first=reset_first, xr_col=0, gr_col=d_rg)
    x1, u2 = _out_proj(out_a, out_b.reshape(n, d_rg), w["w_out"], x2, w["g_post_mix"], w["g_pre_ffn"],
                       tm=min(n, ROW_RESIDENT_TM), tk=OUT_PROJ_TK)
    y = _ffn(u2, w["w_gate"], w["w_up"], w["w_down"], x1, w["g_post_ffn"],
             tm=min(n, ROW_RESIDENT_TM), tf=FFN_TF)
    return y.reshape(batch, seq, d), c1, n1, m1, h1, conv1


def kernel(x_prompt, x_sample, state_mlstm_C, state_mlstm_n, state_mlstm_m, state_rglru_h, state_rglru_conv, g_pre_mix, w_in, b_igate, b_fgate, g_mlstm_head, conv_w, conv_b, w_rg_a, b_rg_a, w_rg_x, b_rg_x, rg_lambda, w_out, g_post_mix, g_pre_ffn, w_ffn_gate, w_ffn_up, w_ffn_down, g_post_ffn):
    depth = w_in.shape[0]
    yp, ys = x_prompt, x_sample
    prompt_states, sample_states = [], []
    for l in range(depth):
        w = _prepare_weights(
            g_pre_mix[l], w_in[l], b_igate[l], b_fgate[l], g_mlstm_head[l], conv_w[l], conv_b[l],
            w_rg_a[l], b_rg_a[l], w_rg_x[l], b_rg_x[l], rg_lambda[l], w_out[l], g_post_mix[l],
            g_pre_ffn[l], w_ffn_gate[l], w_ffn_up[l], w_ffn_down[l], g_post_ffn[l])
        yp, *st = _layer(yp, None, None, w, reset_first=True)
        prompt_states.append(st)
        ys, *st = _layer(
            ys, (state_mlstm_C[l].astype(F32), state_mlstm_n[l].astype(F32), state_mlstm_m[l].astype(F32)),
            (state_rglru_h[l].astype(F32), state_rglru_conv[l].astype(F32)), w, reset_first=False)
        sample_states.append(st)
    stack = lambda states, i: jnp.stack([s[i] for s in states])
    return (yp, ys,
            *(stack(prompt_states, i) for i in range(5)),
            *(stack(sample_states, i) for i in range(5)))
```

```python
import functools

import jax
import jax.numpy as jnp
from jax import lax
from jax.experimental import pallas as pl
from jax.experimental.pallas import tpu as pltpu

F32 = jnp.float32
BF16 = jnp.bfloat16

NORM_EPS = 1e-6
MLSTM_HEADS = 4
RG_BLOCKS = 16
CONV_WIDTH = 4
RG_C = 8.0
MLSTM_CHUNK = 256

LANES = 128
SUBLANES = 8
VMEM_LIMIT_BYTES = 56 * 1024 * 1024

NORM_ROWS = 256
IN_PROJ_TM = 1024
IN_PROJ_TN = 1024
ROW_RESIDENT_TM = 512
OUT_PROJ_TK = 512
FFN_TF = 256
CAST_BLOCK_ROWS = {"w_gate": 64, "w_up": 64, "w_down": 256, "w_out": 128}
RGLRU_TT = 128
RGLRU_CB = 1024


def _params(semantics):
    return pltpu.CompilerParams(dimension_semantics=semantics, vmem_limit_bytes=VMEM_LIMIT_BYTES)


def _rms(y):
    return y * lax.rsqrt(jnp.mean(y * y, axis=-1, keepdims=True) + NORM_EPS)


def _log_sigmoid(x):
    return jnp.minimum(x, 0.0) - jnp.log1p(jnp.exp(-jnp.abs(x)))


def _softplus(x):
    return jnp.maximum(x, 0.0) + jnp.log1p(jnp.exp(-jnp.abs(x)))


def _sigmoid(x):
    return 1.0 / (1.0 + jnp.exp(-x))


def _sigmoid_tanh(x):
    return 0.5 * jnp.tanh(0.5 * x) + 0.5


def _gelu_tanh(x):
    return 0.5 * x * (1.0 + jnp.tanh(0.7978845608028654 * (x + 0.044715 * (x * x * x))))


def _norm_gates_kernel(x_ref, g_ref, wg_ref, bias_ref, u_ref, gates_ref, *, heads):
    u = (_rms(x_ref[...]) * g_ref[...]).astype(BF16)
    u_ref[...] = u
    gates = jnp.dot(u, wg_ref[...], preferred_element_type=F32) + bias_ref[...]
    lane = lax.broadcasted_iota(jnp.int32, gates.shape, 1)
    is_forget = (lane >= heads) & (lane < 2 * heads)
    gates_ref[...] = jnp.where(is_forget, _log_sigmoid(gates), gates)


def _norm_gates(x, g, w_gates, gate_bias, *, rows):
    n, d = x.shape
    return pl.pallas_call(
        functools.partial(_norm_gates_kernel, heads=MLSTM_HEADS),
        grid=(n // rows,),
        in_specs=[
            pl.BlockSpec((rows, d), lambda i: (i, 0)),
            pl.BlockSpec((1, d), lambda i: (0, 0)),
            pl.BlockSpec((d, LANES), lambda i: (0, 0)),
            pl.BlockSpec((1, LANES), lambda i: (0, 0)),
        ],
        out_specs=[
            pl.BlockSpec((rows, d), lambda i: (i, 0)),
            pl.BlockSpec((rows, LANES), lambda i: (i, 0)),
        ],
        out_shape=[jax.ShapeDtypeStruct((n, d), BF16), jax.ShapeDtypeStruct((n, LANES), F32)],
        compiler_params=_params(("parallel",)),
        name="norm_gates",
    )(x, g, w_gates, gate_bias)


class _CastJob:
    def __init__(self, src, block_rows, steps, step_of):
        rows, cols = src.shape
        assert rows % block_rows == 0 and rows // block_rows <= steps, (src.shape, block_rows, steps)
        last = rows // block_rows - 1
        self.src = src
        self.spec = pl.BlockSpec((block_rows, cols), lambda *ids: (jnp.minimum(step_of(*ids), last), 0))
        self.out_shape = jax.ShapeDtypeStruct(src.shape, BF16)

    @staticmethod
    def run(src_ref, dst_ref):
        dst_ref[...] = src_ref[...].astype(dst_ref.dtype)


def _matmul_kernel(a_ref, w_ref, *rest):
    if len(rest) == 3:
        cast_src, o_ref, cast_dst = rest
        _CastJob.run(cast_src, cast_dst)
    else:
        (o_ref,) = rest
    o_ref[...] = jnp.dot(a_ref[...], w_ref[...], preferred_element_type=F32).astype(o_ref.dtype)


def _matmul(a, w, *, ncols, tm, tn, out_dtype, name, cast=None):
    m, k = a.shape
    grid = (m // tm, ncols // tn)
    in_specs = [
        pl.BlockSpec((tm, k), lambda i, j: (i, 0)),
        pl.BlockSpec((k, tn), lambda i, j: (0, j)),
    ]
    out_specs = [pl.BlockSpec((tm, tn), lambda i, j: (i, j))]
    out_shape = [jax.ShapeDtypeStruct((m, ncols), out_dtype)]
    args = [a, w]
    if cast is not None:
        job = _CastJob(*cast, steps=grid[0] * grid[1], step_of=lambda i, j: i * grid[1] + j)
        in_specs.append(job.spec)
        out_specs.append(job.spec)
        out_shape.append(job.out_shape)
        args.append(job.src)
    outs = pl.pallas_call(
        _matmul_kernel,
        grid=grid,
        in_specs=in_specs,
        out_specs=out_specs,
        out_shape=out_shape,
        compiler_params=_params(("arbitrary", "arbitrary") if cast is not None else ("parallel", "arbitrary")),
        name=name,
    )(*args)
    return tuple(outs)


def _mlstm_kernel(*refs, heads, dk, dv, has_state, has_cast):
    refs = list(refs)
    c_scr, n_scr, m_scr = refs[-3:]
    del refs[-3:]
    if has_cast:
        _CastJob.run(refs.pop(6 + 3 * has_state), refs.pop())
    if has_state:
        (q_ref, k_ref, v_ref, o_ref, gates_ref, ghead_ref, c0_ref, n0_ref, m0_ref,
         out_ref, c_out, n_out, m_out) = refs
    else:
        q_ref, k_ref, v_ref, o_ref, gates_ref, ghead_ref, out_ref, c_out, n_out, m_out = refs
    chunk = pl.program_id(1)
    length = q_ref.shape[0]

    @pl.when(chunk == 0)
    def _init():
        if has_state:
            c_scr[...] = c0_ref[...]
            n_scr[...] = n0_ref[...]
            m_scr[...] = m0_ref[...]
        else:
            c_scr[...] = jnp.zeros_like(c_scr)
            n_scr[...] = jnp.zeros_like(n_scr)
            m_scr[...] = jnp.zeros_like(m_scr)

    row = lax.broadcasted_iota(jnp.int32, (length, length), 0)
    col = lax.broadcasted_iota(jnp.int32, (length, length), 1)
    causal = col <= row
    eye = col == row
    gates = gates_ref[...]

    for h in range(heads):
        ig_col = gates[:, h:h + 1]
        lf_col = gates[:, heads + h:heads + h + 1]
        b_row = jnp.sum(jnp.where(row <= col, lf_col, 0.0), axis=0, keepdims=True)
        b_col = jnp.sum(jnp.where(eye, b_row, 0.0), axis=1, keepdims=True)
        ig_row = jnp.sum(jnp.where(eye, ig_col, 0.0), axis=0, keepdims=True)
        log_d = jnp.where(causal, (b_col - b_row) + ig_row, -jnp.inf)
        m_prev = m_scr[h:h + 1, 0:1]
        inter = b_col + m_prev
        m_tok = jnp.maximum(inter, jnp.max(log_d, axis=1, keepdims=True))
        d_mat = jnp.exp(log_d - m_tok)
        w_inter = jnp.exp(inter - m_tok)

        q = q_ref[:, h * dk:(h + 1) * dk] * (dk ** -0.5)
        k = k_ref[:, h * dk:(h + 1) * dk]
        qb = q.astype(BF16)
        vb = v_ref[:, h * dv:(h + 1) * dv].astype(BF16)
        s = lax.dot_general(qb, k.astype(BF16), (((1,), (1,)), ((), ())),
                            preferred_element_type=F32) * d_mat
        c_old = c_scr[h]
        n_old = n_scr[h:h + 1, :]
        num = (w_inter * jnp.dot(qb, c_old.astype(BF16), preferred_element_type=F32)
               + jnp.dot(s.astype(BF16), vb, preferred_element_type=F32))
        den = (w_inter * jnp.sum(q * n_old, axis=1, keepdims=True)
               + jnp.sum(s, axis=1, keepdims=True))
        hid = num / jnp.maximum(jnp.abs(den), jnp.exp(-m_tok))

        m_new = m_tok[length - 1:length, :]
        b_last = b_col[length - 1:length, :]
        w_c = jnp.exp(b_last + m_prev - m_new)
        w_s = jnp.exp((b_last - b_col) + ig_col - m_new)
        kw = k * w_s
        c_scr[h] = w_c * c_old + lax.dot_general(
            kw.astype(BF16), vb, (((0,), (0,)), ((), ())), preferred_element_type=F32)
        n_scr[h:h + 1, :] = w_c * n_old + jnp.sum(kw, axis=0, keepdims=True)
        m_scr[h:h + 1, :] = jnp.broadcast_to(m_new, (1, m_scr.shape[1]))

        normed = _rms(hid) * ghead_ref[:, h * dv:(h + 1) * dv]
        out_ref[:, h * dv:(h + 1) * dv] = (
            _sigmoid_tanh(o_ref[:, h * dv:(h + 1) * dv]) * normed).astype(out_ref.dtype)

    @pl.when(chunk == pl.num_programs(1) - 1)
    def _emit_state():
        c_out[...] = c_scr[...]
        n_out[...] = n_scr[...]
        m_out[...] = m_scr[...]


def _mlstm(proj, gates, g_head, state, *, batch, seq, cast=None):
    heads = MLSTM_HEADS
    d_mlstm = g_head.shape[-1]
    dv = d_mlstm // heads
    dk = dv // 2
    length = min(MLSTM_CHUNK, seq)
    nc = seq // length
    has_state = state is not None

    def rows(b, c):
        return b * nc + c

    in_specs = [
        pl.BlockSpec((length, heads * dk), lambda b, c: (rows(b, c), 0)),
        pl.BlockSpec((length, heads * dk), lambda b, c: (rows(b, c), 1)),
        pl.BlockSpec((length, d_mlstm), lambda b, c: (rows(b, c), 1)),
        pl.BlockSpec((length, d_mlstm), lambda b, c: (rows(b, c), 2)),
        pl.BlockSpec((length, LANES), lambda b, c: (rows(b, c), 0)),
        pl.BlockSpec((1, d_mlstm), lambda b, c: (0, 0)),
    ]
    args = [proj, proj, proj, proj, gates, g_head]
    if has_state:
        c0, n0, m0 = state
        in_specs += [
            pl.BlockSpec((None, heads, dk, dv), lambda b, c: (b, 0, 0, 0)),
            pl.BlockSpec((None, heads, dk), lambda b, c: (b, 0, 0)),
            pl.BlockSpec((None, heads, LANES), lambda b, c: (b, 0, 0)),
        ]
        args += [c0, n0, jnp.broadcast_to(m0[..., None], (batch, heads, LANES))]
    out_specs = [
        pl.BlockSpec((length, d_mlstm), lambda b, c: (rows(b, c), 0)),
        pl.BlockSpec((None, heads, dk, dv), lambda b, c: (b, 0, 0, 0)),
        pl.BlockSpec((None, heads, dk), lambda b, c: (b, 0, 0)),
        pl.BlockSpec((None, heads, LANES), lambda b, c: (b, 0, 0)),
    ]
    out_shape = [
        jax.ShapeDtypeStruct((batch * seq, d_mlstm), BF16),
        jax.ShapeDtypeStruct((batch, heads, dk, dv), F32),
        jax.ShapeDtypeStruct((batch, heads, dk), F32),
        jax.ShapeDtypeStruct((batch, heads, LANES), F32),
    ]
    if cast is not None:
        job = _CastJob(*cast, steps=batch * nc, step_of=rows)
        in_specs.append(job.spec)
        out_specs.append(job.spec)
        out_shape.append(job.out_shape)
        args.append(job.src)
    out, c1, n1, m1, *cast_out = pl.pallas_call(
        functools.partial(_mlstm_kernel, heads=heads, dk=dk, dv=dv, has_state=has_state,
                          has_cast=cast is not None),
        grid=(batch, nc),
        in_specs=in_specs,
        out_specs=out_specs,
        out_shape=out_shape,
        scratch_shapes=[
            pltpu.VMEM((heads, dk, dv), F32),
            pltpu.VMEM((heads, dk), F32),
            pltpu.VMEM((heads, LANES), F32),
        ],
        compiler_params=_params(("arbitrary", "arbitrary") if cast is not None else ("parallel", "arbitrary")),
        name="mlstm",
    )(*args)
    return (out, c1, n1, m1[..., 0], *cast_out)


def _rglru_kernel(*refs, reset_first, has_state, has_cast):
    refs = list(refs)
    if has_cast:
        _CastJob.run(refs.pop(8 + 2 * has_state), refs.pop(-5))
    if has_state:
        (x_ref, gr_ref, convw_ref, convb_ref, wcat_ref, bga_ref, bgx_ref, lam_ref, tail0_ref, h0_ref,
         out_ref, hlast_ref, tail_out_ref, xs_scr, a_scr, b_scr, h_scr) = refs
    else:
        (x_ref, gr_ref, convw_ref, convb_ref, wcat_ref, bga_ref, bgx_ref, lam_ref,
         out_ref, hlast_ref, tail_out_ref, xs_scr, a_scr, b_scr, h_scr) = refs
    t_idx = pl.program_id(2)
    nb, tt, cb = x_ref.shape
    rows = nb * tt
    halo = (CONV_WIDTH - 1) * nb

    nblk = cb // LANES
    blocks = [slice(blk * LANES, (blk + 1) * LANES) for blk in range(nblk)]

    @pl.when(t_idx == 0)
    def _init():
        for blk, lanes in enumerate(blocks):
            if has_state:
                xs_scr[blk, 0:halo, :] = tail0_ref[:, lanes]
                h_scr[blk] = h0_ref[:, lanes]
            else:
                xs_scr[blk, 0:halo, :] = jnp.zeros((halo, LANES), F32)
                h_scr[blk] = jnp.zeros((nb, LANES), F32)

    @pl.when(t_idx > 0)
    def _carry_history():
        xs_scr[:, 0:halo, :] = xs_scr[:, rows:rows + halo, :]

    row = lax.broadcasted_iota(jnp.int32, (rows, LANES), 0)
    first_token = (row < nb) & (t_idx == 0)
    for blk, lanes in enumerate(blocks):
        for s in range(nb):
            xs_scr[blk, pl.ds(halo + s, tt, stride=nb), :] = x_ref[s, :, lanes]
        xc = convb_ref[:, lanes]
        for tap in range(CONV_WIDTH):
            xc = xc + xs_scr[blk, tap * nb:tap * nb + rows, :] * convw_ref[tap:tap + 1, lanes]
        pre = jnp.dot(xc.astype(BF16), wcat_ref[blk], preferred_element_type=F32)
        r_gate = _sigmoid_tanh(pre[:, :LANES] + bga_ref[:, lanes])
        i_gate = _sigmoid_tanh(pre[:, LANES:] + bgx_ref[:, lanes])
        log_a = -RG_C * r_gate * _softplus(-lam_ref[:, lanes])
        a = jnp.exp(log_a)
        gap = 1.0 - a * a
        mult = jnp.where(gap > 0.0, gap * lax.rsqrt(gap), 0.0)
        if reset_first:
            mult = jnp.where(first_token, 1.0, mult)
            a = jnp.where(first_token, 0.0, a)
        a_scr[blk] = a
        b_scr[blk] = mult * (i_gate * xc)

    def scan_step(s, h):
        rows_s = pl.ds(pl.multiple_of(s * nb, nb), nb)
        h = a_scr[:, rows_s, :] * h + b_scr[:, rows_s, :]
        b_scr[:, rows_s, :] = h
        return h

    h_scr[...] = lax.fori_loop(0, tt, scan_step, h_scr[...], unroll=8)

    for blk, lanes in enumerate(blocks):
        for s in range(nb):
            h_stream = b_scr[blk, pl.ds(s, tt, stride=nb), :]
            out_ref[s, :, lanes] = (h_stream * _gelu_tanh(gr_ref[s, :, lanes])).astype(out_ref.dtype)

    @pl.when(t_idx == pl.num_programs(2) - 1)
    def _emit_state():
        for blk, lanes in enumerate(blocks):
            hlast_ref[:, lanes] = h_scr[blk]
            tail_out_ref[:, lanes] = xs_scr[blk, rows:rows + halo, :]


def _rglru(proj3, rg, state, *, reset_first, xr_col, gr_col, cast=None):
    conv_w, conv_b, w_cat, b_ga, b_gx, lam = rg
    batch, seq, _ = proj3.shape
    d_rg = conv_b.shape[-1]
    nb = SUBLANES
    groups = batch // nb
    cb = RGLRU_CB
    tt = min(RGLRU_TT, seq)
    nt = seq // tt
    ncb = d_rg // cb
    xr_off = xr_col // cb
    gr_off = gr_col // cb
    halo = (CONV_WIDTH - 1) * nb
    has_state = state is not None

    def vec(rows):
        return pl.BlockSpec((rows, cb), lambda g, j, t: (0, j))

    in_specs = [
        pl.BlockSpec((nb, tt, cb), lambda g, j, t: (g, t, xr_off + j)),
        pl.BlockSpec((nb, tt, cb), lambda g, j, t: (g, t, gr_off + j)),
        vec(CONV_WIDTH), vec(1),
        pl.BlockSpec((cb // LANES, LANES, 2 * LANES), lambda g, j, t: (j, 0, 0)),
        vec(1), vec(1), vec(1),
    ]
    args = [proj3, proj3, conv_w, conv_b, w_cat, b_ga, b_gx, lam]
    if has_state:
        h0, conv0 = state
        tail0 = conv0.reshape(groups, nb, CONV_WIDTH - 1, d_rg).transpose(0, 2, 1, 3).reshape(groups, halo, d_rg)
        in_specs += [
            pl.BlockSpec((None, halo, cb), lambda g, j, t: (g, 0, j)),
            pl.BlockSpec((nb, cb), lambda g, j, t: (g, j)),
        ]
        args += [tail0, h0]
    out_specs = [
        pl.BlockSpec((nb, tt, cb), lambda g, j, t: (g, t, j)),
        pl.BlockSpec((nb, cb), lambda g, j, t: (g, j)),
        pl.BlockSpec((None, halo, cb), lambda g, j, t: (g, 0, j)),
    ]
    out_shape = [
        jax.ShapeDtypeStruct((batch, seq, d_rg), BF16),
        jax.ShapeDtypeStruct((batch, d_rg), F32),
        jax.ShapeDtypeStruct((groups, halo, d_rg), F32),
    ]
    if cast is not None:
        job = _CastJob(*cast, steps=groups * ncb * nt, step_of=lambda g, j, t: (g * ncb + j) * nt + t)
        in_specs.append(job.spec)
        out_specs.append(job.spec)
        out_shape.append(job.out_shape)
        args.append(job.src)
    out, h_last, tail, *cast_out = pl.pallas_call(
        functools.partial(_rglru_kernel, reset_first=reset_first, has_state=has_state,
                          has_cast=cast is not None),
        grid=(groups, ncb, nt),
        in_specs=in_specs,
        out_specs=out_specs,
        out_shape=out_shape,
        scratch_shapes=[
            pltpu.VMEM((cb // LANES, nb * tt + halo, LANES), F32),
            pltpu.VMEM((cb // LANES, nb * tt, LANES), F32),
            pltpu.VMEM((cb // LANES, nb * tt, LANES), F32),
            pltpu.VMEM((cb // LANES, nb, LANES), F32),
        ],
        compiler_params=_params(("arbitrary",) * 3 if cast is not None else ("parallel", "parallel", "arbitrary")),
        name="rglru",
    )(*args)
    conv1 = tail.reshape(groups, CONV_WIDTH - 1, nb, d_rg).transpose(0, 2, 1, 3).reshape(
        batch, CONV_WIDTH - 1, d_rg)
    return (out, h_last, conv1, *cast_out)


def _norm_rows(src_ref, fn, *, chunk):
    n = src_ref.shape[0] // chunk

    def body(i, carry):
        rows = pl.ds(pl.multiple_of(i * chunk, chunk), chunk)
        fn(rows, src_ref[rows, :])
        return carry

    lax.fori_loop(0, n, body, 0)


def _out_proj_kernel(a_ref, b_ref, w_ref, res_ref, g1_ref, g2_ref, x1_ref, u2_ref, *, nka):
    k = pl.program_id(1)

    @pl.when(k == 0)
    def _zero():
        x1_ref[...] = jnp.zeros_like(x1_ref)

    @pl.when(k < nka)
    def _from_a():
        x1_ref[...] += jnp.dot(a_ref[...], w_ref[...], preferred_element_type=F32)

    @pl.when(k >= nka)
    def _from_b():
        x1_ref[...] += jnp.dot(b_ref[...], w_ref[...], preferred_element_type=F32)

    @pl.when(k == pl.num_programs(1) - 1)
    def _epilogue():
        def fn(rows, y):
            x1 = res_ref[rows, :] + _rms(y) * g1_ref[...]
            x1_ref[rows, :] = x1
            u2_ref[rows, :] = (_rms(x1) * g2_ref[...]).astype(u2_ref.dtype)

        _norm_rows(x1_ref, fn, chunk=64)


def _out_proj(a, b, w, res, g1, g2, *, tm, tk):
    m, ka = a.shape
    kb = b.shape[1]
    n = w.shape[1]
    nka = ka // tk
    nk = nka + kb // tk
    return pl.pallas_call(
        functools.partial(_out_proj_kernel, nka=nka),
        grid=(m // tm, nk),
        in_specs=[
            pl.BlockSpec((tm, tk), lambda i, k: (i, jnp.minimum(k, nka - 1))),
            pl.BlockSpec((tm, tk), lambda i, k: (i, jnp.maximum(k - nka, 0))),
            pl.BlockSpec((tk, n), lambda i, k: (k, 0)),
            pl.BlockSpec((tm, n), lambda i, k: (i, 0)),
            pl.BlockSpec((1, n), lambda i, k: (0, 0)),
            pl.BlockSpec((1, n), lambda i, k: (0, 0)),
        ],
        out_specs=[
            pl.BlockSpec((tm, n), lambda i, k: (i, 0)),
            pl.BlockSpec((tm, n), lambda i, k: (i, 0)),
        ],
        out_shape=[jax.ShapeDtypeStruct((m, n), F32), jax.ShapeDtypeStruct((m, n), BF16)],
        compiler_params=_params(("parallel", "arbitrary")),
        name="out_proj",
    )(a, b, w, res, g1, g2)


def _ffn_kernel(u_ref, wg_ref, wu_ref, wd_ref, res_ref, g_ref, o_ref):
    j = pl.program_id(1)

    @pl.when(j == 0)
    def _zero():
        o_ref[...] = jnp.zeros_like(o_ref)

    u = u_ref[...]
    gate = jnp.dot(u, wg_ref[...], preferred_element_type=F32)
    up = jnp.dot(u, wu_ref[...], preferred_element_type=F32)
    hidden = (gate * _sigmoid(gate) * up).astype(BF16)
    o_ref[...] += jnp.dot(hidden, wd_ref[...], preferred_element_type=F32)

    @pl.when(j == pl.num_programs(1) - 1)
    def _epilogue():
        def fn(rows, y):
            o_ref[rows, :] = res_ref[rows, :] + _rms(y) * g_ref[...]

        _norm_rows(o_ref, fn, chunk=64)


def _ffn(u, wg, wu, wd, res, g, *, tm, tf):
    m, d = u.shape
    f = wg.shape[1]
    return pl.pallas_call(
        _ffn_kernel,
        grid=(m // tm, f // tf),
        in_specs=[
            pl.BlockSpec((tm, d), lambda i, j: (i, 0)),
            pl.BlockSpec((d, tf), lambda i, j: (0, j)),
            pl.BlockSpec((d, tf), lambda i, j: (0, j)),
            pl.BlockSpec((tf, d), lambda i, j: (j, 0)),
            pl.BlockSpec((tm, d), lambda i, j: (i, 0)),
            pl.BlockSpec((1, d), lambda i, j: (0, 0)),
        ],
        out_specs=pl.BlockSpec((tm, d), lambda i, j: (i, 0)),
        out_shape=jax.ShapeDtypeStruct((m, d), F32),
        compiler_params=_params(("parallel", "arbitrary")),
        name="ffn",
    )(u, wg, wu, wd, res, g)


def _prepare_weights(g_pre_mix, w_in, b_igate, b_fgate, g_mlstm_head, conv_w, conv_b, w_rg_a, b_rg_a,
                     w_rg_x, b_rg_x, rg_lambda, w_out, g_post_mix, g_pre_ffn, w_ffn_gate, w_ffn_up,
                     w_ffn_down, g_post_ffn):
    heads = MLSTM_HEADS
    d_mlstm = g_mlstm_head.shape[-1]
    n_dense = heads * (d_mlstm // heads // 2) * 2 + 2 * d_mlstm
    gate_lo, gate_hi = n_dense, n_dense + 2 * heads
    w_in_bf = w_in.astype(BF16)
    w_rg = w_in_bf[:, gate_hi:]
    w_gates = jnp.pad(w_in_bf[:, gate_lo:gate_hi], ((0, 0), (0, LANES - 2 * heads)))
    gate_bias = jnp.pad(jnp.concatenate([b_igate, b_fgate]), (0, LANES - 2 * heads))[None, :]
    row = lambda v: v[None, :].astype(F32)
    rg = (conv_w.astype(F32), row(conv_b),
          jnp.concatenate([w_rg_a, w_rg_x], axis=-1).astype(BF16),
          row(b_rg_a), row(b_rg_x), row(rg_lambda))
    return dict(
        g_pre_mix=row(g_pre_mix), w_in_bf=w_in_bf, n_dense=n_dense, w_rg=w_rg, w_gates=w_gates,
        gate_bias=gate_bias.astype(F32), g_head=row(g_mlstm_head), rg=rg, w_out=w_out,
        g_post_mix=row(g_post_mix), g_pre_ffn=row(g_pre_ffn), w_gate=w_ffn_gate,
        w_up=w_ffn_up, w_down=w_ffn_down, g_post_ffn=row(g_post_ffn))


def _layer(x, mlstm_state, rglru_state, w, *, reset_first):
    batch, seq, d = x.shape
    n = batch * seq
    x2 = x.reshape(n, d)
    d_rg = w["rg"][1].shape[-1]
    tm = min(n, IN_PROJ_TM)

    def job(name):
        return (w[name], CAST_BLOCK_ROWS[name]) if w[name].dtype != BF16 else None

    w = dict(w)
    u, gates = _norm_gates(x2, w["g_pre_mix"], w["w_gates"], w["gate_bias"], rows=NORM_ROWS)
    proj_dense, *cast = _matmul(u, w["w_in_bf"], ncols=w["n_dense"], tm=tm, tn=IN_PROJ_TN, out_dtype=F32,
                                name="in_proj_dense", cast=job("w_gate"))
    w["w_gate"], = cast or [w["w_gate"]]
    proj_rg, *cast = _matmul(u, w["w_rg"], ncols=2 * d_rg, tm=tm, tn=IN_PROJ_TN, out_dtype=F32,
                             name="in_proj_rg", cast=job("w_up"))
    w["w_up"], = cast or [w["w_up"]]
    out_a, c1, n1, m1, *cast = _mlstm(proj_dense, gates, w["g_head"], mlstm_state, batch=batch, seq=seq,
                                      cast=job("w_down"))
    w["w_down"], = cast or [w["w_down"]]
    out_b, h1, conv1, *cast = _rglru(proj_rg.reshape(batch, seq, 2 * d_rg), w["rg"], rglru_state,
                                     reset_first=reset_first, xr_col=0, gr_col=d_rg, cast=job("w_out"))
    w["w_out"], = cast or [w["w_out"]]
    x1, u2 = _out_proj(out_a, out_b.reshape(n, d_rg), w["w_out"], x2, w["g_post_mix"], w["g_pre_ffn"],
                       tm=min(n, ROW_RESIDENT_TM), tk=OUT_PROJ_TK)
    y = _ffn(u2, w["w_gate"], w["w_up"], w["w_down"], x1, w["g_post_ffn"],
             tm=min(n, ROW_RESIDENT_TM), tf=FFN_TF)
    return w, (y.reshape(batch, seq, d), c1, n1, m1, h1, conv1)


def kernel(x_prompt, x_sample, state_mlstm_C, state_mlstm_n, state_mlstm_m, state_rglru_h, state_rglru_conv, g_pre_mix, w_in, b_igate, b_fgate, g_mlstm_head, conv_w, conv_b, w_rg_a, b_rg_a, w_rg_x, b_rg_x, rg_lambda, w_out, g_post_mix, g_pre_ffn, w_ffn_gate, w_ffn_up, w_ffn_down, g_post_ffn):
    depth = w_in.shape[0]
    yp, ys = x_prompt, x_sample
    prompt_states, sample_states = [], []
    for l in range(depth):
        w = _prepare_weights(
            g_pre_mix[l], w_in[l], b_igate[l], b_fgate[l], g_mlstm_head[l], conv_w[l], conv_b[l],
            w_rg_a[l], b_rg_a[l], w_rg_x[l], b_rg_x[l], rg_lambda[l], w_out[l], g_post_mix[l],
            g_pre_ffn[l], w_ffn_gate[l], w_ffn_up[l], w_ffn_down[l], g_post_ffn[l])
        w, (yp, *st) = _layer(yp, None, None, w, reset_first=True)
        prompt_states.append(st)
        w, (ys, *st) = _layer(
            ys, (state_mlstm_C[l].astype(F32), state_mlstm_n[l].astype(F32), state_mlstm_m[l].astype(F32)),
            (state_rglru_h[l].astype(F32), state_rglru_conv[l].astype(F32)), w, reset_first=False)
        sample_states.append(st)
    stack = lambda states, i: jnp.stack([s[i] for s in states])
    return (yp, ys,
            *(stack(prompt_states, i) for i in range(5)),
            *(stack(sample_states, i) for i in range(5)))
```

```python
import functools

import jax
import jax.numpy as jnp
from jax import lax
from jax.experimental import pallas as pl
from jax.experimental.pallas import tpu as pltpu

F32 = jnp.float32
BF16 = jnp.bfloat16

NORM_EPS = 1e-6
MLSTM_HEADS = 4
RG_BLOCKS = 16
CONV_WIDTH = 4
RG_C = 8.0
MLSTM_CHUNK = 256

LANES = 128
SUBLANES = 8
VMEM_LIMIT_BYTES = 56 * 1024 * 1024

NORM_ROWS = 256
IN_PROJ_TM = 1024
IN_PROJ_TN = 1024
ROW_RESIDENT_TM = 512
OUT_PROJ_TK = 512
FFN_TF = 256
CAST_BLOCK_ROWS = {"w_gate": 64, "w_up": 64, "w_down": 256, "w_out": 128}
RGLRU_TT = 128
RGLRU_CB = 1024


def _params(semantics):
    return pltpu.CompilerParams(dimension_semantics=semantics, vmem_limit_bytes=VMEM_LIMIT_BYTES)


def _rms(y):
    return y * lax.rsqrt(jnp.mean(y * y, axis=-1, keepdims=True) + NORM_EPS)


def _log_sigmoid(x):
    return jnp.minimum(x, 0.0) - jnp.log1p(jnp.exp(-jnp.abs(x)))


def _softplus(x):
    return jnp.maximum(x, 0.0) + jnp.log1p(jnp.exp(-jnp.abs(x)))


def _sigmoid(x):
    return 1.0 / (1.0 + jnp.exp(-x))


def _sigmoid_tanh(x):
    return 0.5 * jnp.tanh(0.5 * x) + 0.5


def _gelu_tanh(x):
    return 0.5 * x * (1.0 + jnp.tanh(0.7978845608028654 * (x + 0.044715 * (x * x * x))))


def _norm_gates_kernel(x_ref, g_ref, wg_ref, bias_ref, u_ref, gates_ref, *, heads):
    u = (_rms(x_ref[...]) * g_ref[...]).astype(BF16)
    u_ref[...] = u
    gates = jnp.dot(u, wg_ref[...], preferred_element_type=F32) + bias_ref[...]
    lane = lax.broadcasted_iota(jnp.int32, gates.shape, 1)
    is_forget = (lane >= heads) & (lane < 2 * heads)
    gates_ref[...] = jnp.where(is_forget, _log_sigmoid(gates), gates)


def _norm_gates(x, g, w_gates, gate_bias, *, rows):
    n, d = x.shape
    return pl.pallas_call(
        functools.partial(_norm_gates_kernel, heads=MLSTM_HEADS),
        grid=(n // rows,),
        in_specs=[
            pl.BlockSpec((rows, d), lambda i: (i, 0)),
            pl.BlockSpec((1, d), lambda i: (0, 0)),
            pl.BlockSpec((d, LANES), lambda i: (0, 0)),
            pl.BlockSpec((1, LANES), lambda i: (0, 0)),
        ],
        out_specs=[
            pl.BlockSpec((rows, d), lambda i: (i, 0)),
            pl.BlockSpec((rows, LANES), lambda i: (i, 0)),
        ],
        out_shape=[jax.ShapeDtypeStruct((n, d), BF16), jax.ShapeDtypeStruct((n, LANES), F32)],
        compiler_params=_params(("parallel",)),
        name="norm_gates",
    )(x, g, w_gates, gate_bias)


class _CastJob:
    def __init__(self, src, block_rows, steps, step_of):
        rows, cols = src.shape
        assert rows % block_rows == 0 and rows // block_rows <= steps, (src.shape, block_rows, steps)
        last = rows // block_rows - 1
        self.src = src
        self.spec = pl.BlockSpec((block_rows, cols), lambda *ids: (jnp.minimum(step_of(*ids), last), 0))
        self.out_shape = jax.ShapeDtypeStruct(src.shape, BF16)

    @staticmethod
    def run(src_ref, dst_ref):
        dst_ref[...] = src_ref[...].astype(dst_ref.dtype)


def _matmul_kernel(a_ref, w_ref, *rest):
    if len(rest) == 3:
        cast_src, o_ref, cast_dst = rest
        _CastJob.run(cast_src, cast_dst)
    else:
        (o_ref,) = rest
    o_ref[...] = jnp.dot(a_ref[...], w_ref[...], preferred_element_type=F32).astype(o_ref.dtype)


def _matmul(a, w, *, ncols, tm, tn, out_dtype, name, cast=None):
    m, k = a.shape
    grid = (m // tm, ncols // tn)
    in_specs = [
        pl.BlockSpec((tm, k), lambda i, j: (i, 0)),
        pl.BlockSpec((k, tn), lambda i, j: (0, j)),
    ]
    out_specs = [pl.BlockSpec((tm, tn), lambda i, j: (i, j))]
    out_shape = [jax.ShapeDtypeStruct((m, ncols), out_dtype)]
    args = [a, w]
    if cast is not None:
        job = _CastJob(*cast, steps=grid[0] * grid[1], step_of=lambda i, j: i * grid[1] + j)
        in_specs.append(job.spec)
        out_specs.append(job.spec)
        out_shape.append(job.out_shape)
        args.append(job.src)
    outs = pl.pallas_call(
        _matmul_kernel,
        grid=grid,
        in_specs=in_specs,
        out_specs=out_specs,
        out_shape=out_shape,
        compiler_params=_params(("arbitrary", "arbitrary") if cast is not None else ("parallel", "arbitrary")),
        name=name,
    )(*args)
    return tuple(outs)


def _mlstm_kernel(*refs, heads, dk, dv, has_state, has_cast):
    refs = list(refs)
    c_scr, n_scr, m_scr = refs[-3:]
    del refs[-3:]
    if has_cast:
        _CastJob.run(refs.pop(6 + 3 * has_state), refs.pop())
    if has_state:
        (q_ref, k_ref, v_ref, o_ref, gates_ref, ghead_ref, c0_ref, n0_ref, m0_ref,
         out_ref, c_out, n_out, m_out) = refs
    else:
        q_ref, k_ref, v_ref, o_ref, gates_ref, ghead_ref, out_ref, c_out, n_out, m_out = refs
    chunk = pl.program_id(1)
    length = q_ref.shape[0]

    @pl.when(chunk == 0)
    def _init():
        if has_state:
            c_scr[...] = c0_ref[...]
            n_scr[...] = n0_ref[...]
            m_scr[...] = m0_ref[...]
        else:
            c_scr[...] = jnp.zeros_like(c_scr)
            n_scr[...] = jnp.zeros_like(n_scr)
            m_scr[...] = jnp.zeros_like(m_scr)

    row = lax.broadcasted_iota(jnp.int32, (length, length), 0)
    col = lax.broadcasted_iota(jnp.int32, (length, length), 1)
    causal = col <= row
    eye = col == row
    gates = gates_ref[...]

    for h in range(heads):
        ig_col = gates[:, h:h + 1]
        lf_col = gates[:, heads + h:heads + h + 1]
        b_row = jnp.sum(jnp.where(row <= col, lf_col, 0.0), axis=0, keepdims=True)
        b_col = jnp.sum(jnp.where(eye, b_row, 0.0), axis=1, keepdims=True)
        ig_row = jnp.sum(jnp.where(eye, ig_col, 0.0), axis=0, keepdims=True)
        log_d = jnp.where(causal, (b_col - b_row) + ig_row, -jnp.inf)
        m_prev = m_scr[h:h + 1, 0:1]
        inter = b_col + m_prev
        m_tok = jnp.maximum(inter, jnp.max(log_d, axis=1, keepdims=True))
        d_mat = jnp.exp(log_d - m_tok)
        w_inter = jnp.exp(inter - m_tok)

        q = q_ref[:, h * dk:(h + 1) * dk] * (dk ** -0.5)
        k = k_ref[:, h * dk:(h + 1) * dk]
        qb = q.astype(BF16)
        vb = v_ref[:, h * dv:(h + 1) * dv].astype(BF16)
        s = lax.dot_general(qb, k.astype(BF16), (((1,), (1,)), ((), ())),
                            preferred_element_type=F32) * d_mat
        c_old = c_scr[h]
        n_old = n_scr[h:h + 1, :]
        num = (w_inter * jnp.dot(qb, c_old.astype(BF16), preferred_element_type=F32)
               + jnp.dot(s.astype(BF16), vb, preferred_element_type=F32))
        den = (w_inter * jnp.sum(q * n_old, axis=1, keepdims=True)
               + jnp.sum(s, axis=1, keepdims=True))
        hid = num / jnp.maximum(jnp.abs(den), jnp.exp(-m_tok))

        m_new = m_tok[length - 1:length, :]
        b_last = b_col[length - 1:length, :]
        w_c = jnp.exp(b_last + m_prev - m_new)
        w_s = jnp.exp((b_last - b_col) + ig_col - m_new)
        kw = k * w_s
        c_scr[h] = w_c * c_old + lax.dot_general(
            kw.astype(BF16), vb, (((0,), (0,)), ((), ())), preferred_element_type=F32)
        n_scr[h:h + 1, :] = w_c * n_old + jnp.sum(kw, axis=0, keepdims=True)
        m_scr[h:h + 1, :] = jnp.broadcast_to(m_new, (1, m_scr.shape[1]))

        normed = _rms(hid) * ghead_ref[:, h * dv:(h + 1) * dv]
        out_ref[:, h * dv:(h + 1) * dv] = (
            _sigmoid_tanh(o_ref[:, h * dv:(h + 1) * dv]) * normed).astype(out_ref.dtype)

    @pl.when(chunk == pl.num_programs(1) - 1)
    def _emit_state():
        c_out[...] = c_scr[...]
        n_out[...] = n_scr[...]
        m_out[...] = m_scr[...]


def _mlstm(proj, gates, g_head, state, *, batch, seq, cast=None):
    heads = MLSTM_HEADS
    d_mlstm = g_head.shape[-1]
    dv = d_mlstm // heads
    dk = dv // 2
    length = min(MLSTM_CHUNK, seq)
    nc = seq // length
    has_state = state is not None

    def rows(b, c):
        return b * nc + c

    in_specs = [
        pl.BlockSpec((length, heads * dk), lambda b, c: (rows(b, c), 0)),
        pl.BlockSpec((length, heads * dk), lambda b, c: (rows(b, c), 1)),
        pl.BlockSpec((length, d_mlstm), lambda b, c: (rows(b, c), 1)),
        pl.BlockSpec((length, d_mlstm), lambda b, c: (rows(b, c), 2)),
        pl.BlockSpec((length, LANES), lambda b, c: (rows(b, c), 0)),
        pl.BlockSpec((1, d_mlstm), lambda b, c: (0, 0)),
    ]
    args = [proj, proj, proj, proj, gates, g_head]
    if has_state:
        c0, n0, m0 = state
        in_specs += [
            pl.BlockSpec((None, heads, dk, dv), lambda b, c: (b, 0, 0, 0)),
            pl.BlockSpec((None, heads, dk), lambda b, c: (b, 0, 0)),
            pl.BlockSpec((None, heads, LANES), lambda b, c: (b, 0, 0)),
        ]
        args += [c0, n0, jnp.broadcast_to(m0[..., None], (batch, heads, LANES))]
    out_specs = [
        pl.BlockSpec((length, d_mlstm), lambda b, c: (rows(b, c), 0)),
        pl.BlockSpec((None, heads, dk, dv), lambda b, c: (b, 0, 0, 0)),
        pl.BlockSpec((None, heads, dk), lambda b, c: (b, 0, 0)),
        pl.BlockSpec((None, heads, LANES), lambda b, c: (b, 0, 0)),
    ]
    out_shape = [
        jax.ShapeDtypeStruct((batch * seq, d_mlstm), BF16),
        jax.ShapeDtypeStruct((batch, heads, dk, dv), F32),
        jax.ShapeDtypeStruct((batch, heads, dk), F32),
        jax.ShapeDtypeStruct((batch, heads, LANES), F32),
    ]
    if cast is not None:
        job = _CastJob(*cast, steps=batch * nc, step_of=rows)
        in_specs.append(job.spec)
        out_specs.append(job.spec)
        out_shape.append(job.out_shape)
        args.append(job.src)
    out, c1, n1, m1, *cast_out = pl.pallas_call(
        functools.partial(_mlstm_kernel, heads=heads, dk=dk, dv=dv, has_state=has_state,
                          has_cast=cast is not None),
        grid=(batch, nc),
        in_specs=in_specs,
        out_specs=out_specs,
        out_shape=out_shape,
        scratch_shapes=[
            pltpu.VMEM((heads, dk, dv), F32),
            pltpu.VMEM((heads, dk), F32),
            pltpu.VMEM((heads, LANES), F32),
        ],
        compiler_params=_params(("arbitrary", "arbitrary") if cast is not None else ("parallel", "arbitrary")),
        name="mlstm",
    )(*args)
    return (out, c1, n1, m1[..., 0], *cast_out)


def _rglru_kernel(*refs, reset_first, has_state, has_cast):
    refs = list(refs)
    if has_cast:
        _CastJob.run(refs.pop(8 + 2 * has_state), refs.pop(-5))
    if has_state:
        (x_ref, gr_ref, convw_ref, convb_ref, wcat_ref, bga_ref, bgx_ref, lam_ref, tail0_ref, h0_ref,
         out_ref, hlast_ref, tail_out_ref, xs_scr, a_scr, b_scr, h_scr) = refs
    else:
        (x_ref, gr_ref, convw_ref, convb_ref, wcat_ref, bga_ref, bgx_ref, lam_ref,
         out_ref, hlast_ref, tail_out_ref, xs_scr, a_scr, b_scr, h_scr) = refs
    t_idx = pl.program_id(2)
    nb, tt, cb = x_ref.shape
    rows = nb * tt
    halo = (CONV_WIDTH - 1) * nb

    nblk = cb // LANES
    blocks = [slice(blk * LANES, (blk + 1) * LANES) for blk in range(nblk)]

    @pl.when(t_idx == 0)
    def _init():
        for blk, lanes in enumerate(blocks):
            if has_state:
                xs_scr[blk, 0:halo, :] = tail0_ref[:, lanes]
                h_scr[blk] = h0_ref[:, lanes]
            else:
                xs_scr[blk, 0:halo, :] = jnp.zeros((halo, LANES), F32)
                h_scr[blk] = jnp.zeros((nb, LANES), F32)

    @pl.when(t_idx > 0)
    def _carry_history():
        xs_scr[:, 0:halo, :] = xs_scr[:, rows:rows + halo, :]

    row = lax.broadcasted_iota(jnp.int32, (rows, LANES), 0)
    first_token = (row < nb) & (t_idx == 0)
    for blk, lanes in enumerate(blocks):
        for s in range(nb):
            xs_scr[blk, pl.ds(halo + s, tt, stride=nb), :] = x_ref[s, :, lanes]
        xc = convb_ref[:, lanes]
        for tap in range(CONV_WIDTH):
            xc = xc + xs_scr[blk, tap * nb:tap * nb + rows, :] * convw_ref[tap:tap + 1, lanes]
        pre = jnp.dot(xc.astype(BF16), wcat_ref[blk], preferred_element_type=F32)
        r_gate = _sigmoid_tanh(pre[:, :LANES] + bga_ref[:, lanes])
        i_gate = _sigmoid_tanh(pre[:, LANES:] + bgx_ref[:, lanes])
        log_a = -RG_C * r_gate * _softplus(-lam_ref[:, lanes])
        a = jnp.exp(log_a)
        gap = -jnp.tanh(log_a) * (1.0 + a * a)
        mult = jnp.where(gap > 0.0, gap * lax.rsqrt(gap), 0.0)
        if reset_first:
            mult = jnp.where(first_token, 1.0, mult)
            a = jnp.where(first_token, 0.0, a)
        a_scr[blk] = a
        b_scr[blk] = mult * (i_gate * xc)

    def scan_step(s, h):
        rows_s = pl.ds(pl.multiple_of(s * nb, nb), nb)
        h = a_scr[:, rows_s, :] * h + b_scr[:, rows_s, :]
        b_scr[:, rows_s, :] = h
        return h

    h_scr[...] = lax.fori_loop(0, tt, scan_step, h_scr[...], unroll=8)

    for blk, lanes in enumerate(blocks):
        for s in range(nb):
            h_stream = b_scr[blk, pl.ds(s, tt, stride=nb), :]
            out_ref[s, :, lanes] = (h_stream * _gelu_tanh(gr_ref[s, :, lanes])).astype(out_ref.dtype)

    @pl.when(t_idx == pl.num_programs(2) - 1)
    def _emit_state():
        for blk, lanes in enumerate(blocks):
            hlast_ref[:, lanes] = h_scr[blk]
            tail_out_ref[:, lanes] = xs_scr[blk, rows:rows + halo, :]


def _rglru(proj3, rg, state, *, reset_first, xr_col, gr_col, cast=None):
    conv_w, conv_b, w_cat, b_ga, b_gx, lam = rg
    batch, seq, _ = proj3.shape
    d_rg = conv_b.shape[-1]
    nb = SUBLANES
    groups = batch // nb
    cb = RGLRU_CB
    tt = min(RGLRU_TT, seq)
    nt = seq // tt
    ncb = d_rg // cb
    xr_off = xr_col // cb
    gr_off = gr_col // cb
    halo = (CONV_WIDTH - 1) * nb
    has_state = state is not None

    def vec(rows):
        return pl.BlockSpec((rows, cb), lambda g, j, t: (0, j))

    in_specs = [
        pl.BlockSpec((nb, tt, cb), lambda g, j, t: (g, t, xr_off + j)),
        pl.BlockSpec((nb, tt, cb), lambda g, j, t: (g, t, gr_off + j)),
        vec(CONV_WIDTH), vec(1),
        pl.BlockSpec((cb // LANES, LANES, 2 * LANES), lambda g, j, t: (j, 0, 0)),
        vec(1), vec(1), vec(1),
    ]
    args = [proj3, proj3, conv_w, conv_b, w_cat, b_ga, b_gx, lam]
    if has_state:
        h0, conv0 = state
        tail0 = conv0.reshape(groups, nb, CONV_WIDTH - 1, d_rg).transpose(0, 2, 1, 3).reshape(groups, halo, d_rg)
        in_specs += [
            pl.BlockSpec((None, halo, cb), lambda g, j, t: (g, 0, j)),
            pl.BlockSpec((nb, cb), lambda g, j, t: (g, j)),
        ]
        args += [tail0, h0]
    out_specs = [
        pl.BlockSpec((nb, tt, cb), lambda g, j, t: (g, t, j)),
        pl.BlockSpec((nb, cb), lambda g, j, t: (g, j)),
        pl.BlockSpec((None, halo, cb), lambda g, j, t: (g, 0, j)),
    ]
    out_shape = [
        jax.ShapeDtypeStruct((batch, seq, d_rg), BF16),
        jax.ShapeDtypeStruct((batch, d_rg), F32),
        jax.ShapeDtypeStruct((groups, halo, d_rg), F32),
    ]
    if cast is not None:
        job = _CastJob(*cast, steps=groups * ncb * nt, step_of=lambda g, j, t: (g * ncb + j) * nt + t)
        in_specs.append(job.spec)
        out_specs.append(job.spec)
        out_shape.append(job.out_shape)
        args.append(job.src)
    out, h_last, tail, *cast_out = pl.pallas_call(
        functools.partial(_rglru_kernel, reset_first=reset_first, has_state=has_state,
                          has_cast=cast is not None),
        grid=(groups, ncb, nt),
        in_specs=in_specs,
        out_specs=out_specs,
        out_shape=out_shape,
        scratch_shapes=[
            pltpu.VMEM((cb // LANES, nb * tt + halo, LANES), F32),
            pltpu.VMEM((cb // LANES, nb * tt, LANES), F32),
            pltpu.VMEM((cb // LANES, nb * tt, LANES), F32),
            pltpu.VMEM((cb // LANES, nb, LANES), F32),
        ],
        compiler_params=_params(("arbitrary",) * 3 if cast is not None else ("parallel", "parallel", "arbitrary")),
        name="rglru",
    )(*args)
    conv1 = tail.reshape(groups, CONV_WIDTH - 1, nb, d_rg).transpose(0, 2, 1, 3).reshape(
        batch, CONV_WIDTH - 1, d_rg)
    return (out, h_last, conv1, *cast_out)


def _norm_rows(src_ref, fn, *, chunk):
    n = src_ref.shape[0] // chunk

    def body(i, carry):
        rows = pl.ds(pl.multiple_of(i * chunk, chunk), chunk)
        fn(rows, src_ref[rows, :])
        return carry

    lax.fori_loop(0, n, body, 0)


def _late_row_specs(tm, d, switch_at):
    width = d // len(switch_at)

    def index_map(i, j, *, part, step):
        return jnp.where(j >= step, i, jnp.maximum(i - 1, 0)), part

    return [pl.BlockSpec((tm, width), functools.partial(index_map, part=p, step=s))
            for p, s in enumerate(switch_at)]


def _read_row_parts(part_refs, rows):
    return jnp.concatenate([r[rows, :] for r in part_refs], axis=-1)


def _out_proj_kernel(a_ref, b_ref, w_ref, *rest, nka, res_parts):
    res_refs = rest[:res_parts]
    g1_ref, g2_ref, x1_ref, u2_ref = rest[res_parts:]
    k = pl.program_id(1)

    @pl.when(k == 0)
    def _first():
        x1_ref[...] = jnp.dot(a_ref[...], w_ref[...], preferred_element_type=F32)

    @pl.when((k > 0) & (k < nka))
    def _from_a():
        x1_ref[...] += jnp.dot(a_ref[...], w_ref[...], preferred_element_type=F32)

    @pl.when(k >= nka)
    def _from_b():
        x1_ref[...] += jnp.dot(b_ref[...], w_ref[...], preferred_element_type=F32)

    @pl.when(k == pl.num_programs(1) - 1)
    def _epilogue():
        def fn(rows, y):
            x1 = _read_row_parts(res_refs, rows) + _rms(y) * g1_ref[...]
            x1_ref[rows, :] = x1
            u2_ref[rows, :] = (_rms(x1) * g2_ref[...]).astype(u2_ref.dtype)

        _norm_rows(x1_ref, fn, chunk=64)


def _out_proj(a, b, w, res, g1, g2, *, tm, tk):
    m, ka = a.shape
    kb = b.shape[1]
    n = w.shape[1]
    nka = ka // tk
    nk = nka + kb // tk
    res_switch = tuple(range(nk // 4, nk // 4 + 4))
    return pl.pallas_call(
        functools.partial(_out_proj_kernel, nka=nka, res_parts=len(res_switch)),
        grid=(m // tm, nk),
        in_specs=[
            pl.BlockSpec((tm, tk), lambda i, k: (i, jnp.minimum(k, nka - 1))),
            pl.BlockSpec((tm, tk), lambda i, k: (i, jnp.maximum(k - nka, 0))),
            pl.BlockSpec((tk, n), lambda i, k: (k, 0)),
            *_late_row_specs(tm, n, res_switch),
            pl.BlockSpec((1, n), lambda i, k: (0, 0)),
            pl.BlockSpec((1, n), lambda i, k: (0, 0)),
        ],
        out_specs=[
            pl.BlockSpec((tm, n), lambda i, k: (i, 0)),
            pl.BlockSpec((tm, n), lambda i, k: (i, 0)),
        ],
        out_shape=[jax.ShapeDtypeStruct((m, n), F32), jax.ShapeDtypeStruct((m, n), BF16)],
        compiler_params=_params(("arbitrary", "arbitrary")),
        name="out_proj",
    )(a, b, w, *([res] * len(res_switch)), g1, g2)


def _ffn_kernel(u_ref, wg_ref, wu_ref, wd_ref, *rest, res_parts):
    res_refs = rest[:res_parts]
    g_ref, o_ref = rest[res_parts:]
    j = pl.program_id(1)

    @pl.when(j == 0)
    def _zero():
        o_ref[...] = jnp.zeros_like(o_ref)

    u = u_ref[...]
    gate = jnp.dot(u, wg_ref[...], preferred_element_type=F32)
    up = jnp.dot(u, wu_ref[...], preferred_element_type=F32)
    hidden = (gate * _sigmoid(gate) * up).astype(BF16)
    o_ref[...] += jnp.dot(hidden, wd_ref[...], preferred_element_type=F32)

    @pl.when(j == pl.num_programs(1) - 1)
    def _epilogue():
        def fn(rows, y):
            o_ref[rows, :] = _read_row_parts(res_refs, rows) + _rms(y) * g_ref[...]

        _norm_rows(o_ref, fn, chunk=64)


def _ffn(u, wg, wu, wd, res, g, *, tm, tf):
    m, d = u.shape
    f = wg.shape[1]
    nj = f // tf
    res_switch = (nj // 3, 2 * nj // 3)
    return pl.pallas_call(
        functools.partial(_ffn_kernel, res_parts=len(res_switch)),
        grid=(m // tm, nj),
        in_specs=[
            pl.BlockSpec((tm, d), lambda i, j: (i, 0)),
            pl.BlockSpec((d, tf), lambda i, j: (0, j)),
            pl.BlockSpec((d, tf), lambda i, j: (0, j)),
            pl.BlockSpec((tf, d), lambda i, j: (j, 0)),
            *_late_row_specs(tm, d, res_switch),
            pl.BlockSpec((1, d), lambda i, j: (0, 0)),
        ],
        out_specs=pl.BlockSpec((tm, d), lambda i, j: (i, 0)),
        out_shape=jax.ShapeDtypeStruct((m, d), F32),
        compiler_params=_params(("arbitrary", "arbitrary")),
        name="ffn",
    )(u, wg, wu, wd, *([res] * len(res_switch)), g)


def _prepare_weights(g_pre_mix, w_in, b_igate, b_fgate, g_mlstm_head, conv_w, conv_b, w_rg_a, b_rg_a,
                     w_rg_x, b_rg_x, rg_lambda, w_out, g_post_mix, g_pre_ffn, w_ffn_gate, w_ffn_up,
                     w_ffn_down, g_post_ffn):
    heads = MLSTM_HEADS
    d_mlstm = g_mlstm_head.shape[-1]
    n_dense = heads * (d_mlstm // heads // 2) * 2 + 2 * d_mlstm
    gate_lo, gate_hi = n_dense, n_dense + 2 * heads
    w_in_bf = w_in.astype(BF16)
    w_rg = w_in_bf[:, gate_hi:]
    w_gates = jnp.pad(w_in_bf[:, gate_lo:gate_hi], ((0, 0), (0, LANES - 2 * heads)))
    gate_bias = jnp.pad(jnp.concatenate([b_igate, b_fgate]), (0, LANES - 2 * heads))[None, :]
    row = lambda v: v[None, :].astype(F32)
    rg = (conv_w.astype(F32), row(conv_b),
          jnp.concatenate([w_rg_a, w_rg_x], axis=-1).astype(BF16),
          row(b_rg_a), row(b_rg_x), row(rg_lambda))
    return dict(
        g_pre_mix=row(g_pre_mix), w_in_bf=w_in_bf, n_dense=n_dense, w_rg=w_rg, w_gates=w_gates,
        gate_bias=gate_bias.astype(F32), g_head=row(g_mlstm_head), rg=rg, w_out=w_out,
        g_post_mix=row(g_post_mix), g_pre_ffn=row(g_pre_ffn), w_gate=w_ffn_gate,
        w_up=w_ffn_up, w_down=w_ffn_down, g_post_ffn=row(g_post_ffn))


def _layer(x, mlstm_state, rglru_state, w, *, reset_first):
    batch, seq, d = x.shape
    n = batch * seq
    x2 = x.reshape(n, d)
    d_rg = w["rg"][1].shape[-1]
    tm = min(n, IN_PROJ_TM)

    def job(name):
        return (w[name], CAST_BLOCK_ROWS[name]) if w[name].dtype != BF16 else None

    w = dict(w)
    u, gates = _norm_gates(x2, w["g_pre_mix"], w["w_gates"], w["gate_bias"], rows=NORM_ROWS)
    proj_dense, *cast = _matmul(u, w["w_in_bf"], ncols=w["n_dense"], tm=tm, tn=IN_PROJ_TN, out_dtype=F32,
                                name="in_proj_dense", cast=job("w_gate"))
    w["w_gate"], = cast or [w["w_gate"]]
    proj_rg, *cast = _matmul(u, w["w_rg"], ncols=2 * d_rg, tm=tm, tn=IN_PROJ_TN, out_dtype=F32,
                             name="in_proj_rg", cast=job("w_up"))
    w["w_up"], = cast or [w["w_up"]]
    out_a, c1, n1, m1, *cast = _mlstm(proj_dense, gates, w["g_head"], mlstm_state, batch=batch, seq=seq,
                                      cast=job("w_down"))
    w["w_down"], = cast or [w["w_down"]]
    out_b, h1, conv1, *cast = _rglru(proj_rg.reshape(batch, seq, 2 * d_rg), w["rg"], rglru_state,
                                     reset_first=reset_first, xr_col=0, gr_col=d_rg, cast=job("w_out"))
    w["w_out"], = cast or [w["w_out"]]
    x1, u2 = _out_proj(out_a, out_b.reshape(n, d_rg), w["w_out"], x2, w["g_post_mix"], w["g_pre_ffn"],
                       tm=min(n, ROW_RESIDENT_TM), tk=OUT_PROJ_TK)
    y = _ffn(u2, w["w_gate"], w["w_up"], w["w_down"], x1, w["g_post_ffn"],
             tm=min(n, ROW_RESIDENT_TM), tf=FFN_TF)
    return w, (y.reshape(batch, seq, d), c1, n1, m1, h1, conv1)


def kernel(x_prompt, x_sample, state_mlstm_C, state_mlstm_n, state_mlstm_m, state_rglru_h, state_rglru_conv, g_pre_mix, w_in, b_igate, b_fgate, g_mlstm_head, conv_w, conv_b, w_rg_a, b_rg_a, w_rg_x, b_rg_x, rg_lambda, w_out, g_post_mix, g_pre_ffn, w_ffn_gate, w_ffn_up, w_ffn_down, g_post_ffn):
    depth = w_in.shape[0]
    yp, ys = x_prompt, x_sample
    prompt_states, sample_states = [], []
    for l in range(depth):
        w = _prepare_weights(
            g_pre_mix[l], w_in[l], b_igate[l], b_fgate[l], g_mlstm_head[l], conv_w[l], conv_b[l],
            w_rg_a[l], b_rg_a[l], w_rg_x[l], b_rg_x[l], rg_lambda[l], w_out[l], g_post_mix[l],
            g_pre_ffn[l], w_ffn_gate[l], w_ffn_up[l], w_ffn_down[l], g_post_ffn[l])
        w, (yp, *st) = _layer(yp, None, None, w, reset_first=True)
        prompt_states.append(st)
        w, (ys, *st) = _layer(
            ys, (state_mlstm_C[l].astype(F32), state_mlstm_n[l].astype(F32), state_mlstm_m[l].astype(F32)),
            (state_rglru_h[l].astype(F32), state_rglru_conv[l].astype(F32)), w, reset_first=False)
        sample_states.append(st)
    stack = lambda states, i: jnp.stack([s[i] for s in states])
    return (yp, ys,
            *(stack(prompt_states, i) for i in range(5)),
            *(stack(sample_states, i) for i in range(5)))
```

```python
import functools

import jax
import jax.numpy as jnp
from jax import lax
from jax.experimental import pallas as pl
from jax.experimental.pallas import tpu as pltpu

F32 = jnp.float32
BF16 = jnp.bfloat16

NORM_EPS = 1e-6
MLSTM_HEADS = 4
RG_BLOCKS = 16
CONV_WIDTH = 4
RG_C = 8.0
MLSTM_CHUNK = 256

LANES = 128
SUBLANES = 8
VMEM_LIMIT_BYTES = 56 * 1024 * 1024

NORM_ROWS = 256
IN_PROJ_TM = 1024
IN_PROJ_TN = 1024
ROW_RESIDENT_TM = 512
OUT_PROJ_TK = 512
FFN_TF = 256
CAST_BLOCK_ROWS = {"w_gate": 64, "w_up": 64, "w_down": 256, "w_out": 128}
RGLRU_TT = 128
RGLRU_CB = 1024


def _params(semantics):
    return pltpu.CompilerParams(dimension_semantics=semantics, vmem_limit_bytes=VMEM_LIMIT_BYTES)


def _rms(y):
    return y * lax.rsqrt(jnp.mean(y * y, axis=-1, keepdims=True) + NORM_EPS)


def _log_sigmoid(x):
    return jnp.minimum(x, 0.0) - jnp.log1p(jnp.exp(-jnp.abs(x)))


def _softplus(x):
    return jnp.maximum(x, 0.0) + jnp.log1p(jnp.exp(-jnp.abs(x)))


def _sigmoid(x):
    return 1.0 / (1.0 + jnp.exp(-x))


def _sigmoid_tanh(x):
    return 0.5 * jnp.tanh(0.5 * x) + 0.5


def _gelu_tanh(x):
    return 0.5 * x * (1.0 + jnp.tanh(0.7978845608028654 * (x + 0.044715 * (x * x * x))))


def _norm_gates_kernel(x_ref, g_ref, wg_ref, bias_ref, u_ref, gates_ref, *, heads):
    u = (_rms(x_ref[...]) * g_ref[...]).astype(BF16)
    u_ref[...] = u
    gates = jnp.dot(u, wg_ref[...], preferred_element_type=F32) + bias_ref[...]
    lane = lax.broadcasted_iota(jnp.int32, gates.shape, 1)
    is_forget = (lane >= heads) & (lane < 2 * heads)
    gates_ref[...] = jnp.where(is_forget, _log_sigmoid(gates), gates)


def _norm_gates(x, g, w_gates, gate_bias, *, rows):
    n, d = x.shape
    return pl.pallas_call(
        functools.partial(_norm_gates_kernel, heads=MLSTM_HEADS),
        grid=(n // rows,),
        in_specs=[
            pl.BlockSpec((rows, d), lambda i: (i, 0)),
            pl.BlockSpec((1, d), lambda i: (0, 0)),
            pl.BlockSpec((d, LANES), lambda i: (0, 0)),
            pl.BlockSpec((1, LANES), lambda i: (0, 0)),
        ],
        out_specs=[
            pl.BlockSpec((rows, d), lambda i: (i, 0)),
            pl.BlockSpec((rows, LANES), lambda i: (i, 0)),
        ],
        out_shape=[jax.ShapeDtypeStruct((n, d), BF16), jax.ShapeDtypeStruct((n, LANES), F32)],
        compiler_params=_params(("parallel",)),
        name="norm_gates",
    )(x, g, w_gates, gate_bias)


class _CastJob:
    def __init__(self, src, block_rows, steps, step_of):
        rows, cols = src.shape
        assert rows % block_rows == 0 and rows // block_rows <= steps, (src.shape, block_rows, steps)
        last = rows // block_rows - 1
        self.src = src
        self.spec = pl.BlockSpec((block_rows, cols), lambda *ids: (jnp.minimum(step_of(*ids), last), 0))
        self.out_shape = jax.ShapeDtypeStruct(src.shape, BF16)

    @staticmethod
    def run(src_ref, dst_ref):
        dst_ref[...] = src_ref[...].astype(dst_ref.dtype)


def _matmul_kernel(a_ref, w_ref, *rest):
    if len(rest) == 3:
        cast_src, o_ref, cast_dst = rest
        _CastJob.run(cast_src, cast_dst)
    else:
        (o_ref,) = rest
    o_ref[...] = jnp.dot(a_ref[...], w_ref[...], preferred_element_type=F32).astype(o_ref.dtype)


def _matmul(a, w, *, ncols, tm, tn, out_dtype, name, cast=None):
    m, k = a.shape
    grid = (m // tm, ncols // tn)
    in_specs = [
        pl.BlockSpec((tm, k), lambda i, j: (i, 0)),
        pl.BlockSpec((k, tn), lambda i, j: (0, j)),
    ]
    out_specs = [pl.BlockSpec((tm, tn), lambda i, j: (i, j))]
    out_shape = [jax.ShapeDtypeStruct((m, ncols), out_dtype)]
    args = [a, w]
    if cast is not None:
        job = _CastJob(*cast, steps=grid[0] * grid[1], step_of=lambda i, j: i * grid[1] + j)
        in_specs.append(job.spec)
        out_specs.append(job.spec)
        out_shape.append(job.out_shape)
        args.append(job.src)
    outs = pl.pallas_call(
        _matmul_kernel,
        grid=grid,
        in_specs=in_specs,
        out_specs=out_specs,
        out_shape=out_shape,
        compiler_params=_params(("arbitrary", "arbitrary") if cast is not None else ("parallel", "arbitrary")),
        name=name,
    )(*args)
    return tuple(outs)


def _mlstm_kernel(*refs, heads, dk, dv, has_state, has_cast):
    refs = list(refs)
    c_scr, n_scr, m_scr = refs[-3:]
    del refs[-3:]
    if has_cast:
        _CastJob.run(refs.pop(6 + 3 * has_state), refs.pop())
    if has_state:
        (q_ref, k_ref, v_ref, o_ref, gates_ref, ghead_ref, c0_ref, n0_ref, m0_ref,
         out_ref, c_out, n_out, m_out) = refs
    else:
        q_ref, k_ref, v_ref, o_ref, gates_ref, ghead_ref, out_ref, c_out, n_out, m_out = refs
    chunk = pl.program_id(1)
    length = q_ref.shape[0]

    @pl.when(chunk == 0)
    def _init():
        if has_state:
            c_scr[...] = c0_ref[...]
            n_scr[...] = n0_ref[...]
            m_scr[...] = m0_ref[...]
        else:
            c_scr[...] = jnp.zeros_like(c_scr)
            n_scr[...] = jnp.zeros_like(n_scr)
            m_scr[...] = jnp.zeros_like(m_scr)

    row = lax.broadcasted_iota(jnp.int32, (length, length), 0)
    col = lax.broadcasted_iota(jnp.int32, (length, length), 1)
    causal = col <= row
    eye = col == row
    gates = gates_ref[...]

    for h in range(heads):
        ig_col = gates[:, h:h + 1]
        lf_col = gates[:, heads + h:heads + h + 1]
        b_row = jnp.sum(jnp.where(row <= col, lf_col, 0.0), axis=0, keepdims=True)
        b_col = jnp.sum(jnp.where(eye, b_row, 0.0), axis=1, keepdims=True)
        ig_row = jnp.sum(jnp.where(eye, ig_col, 0.0), axis=0, keepdims=True)
        log_d = jnp.where(causal, (b_col - b_row) + ig_row, -jnp.inf)
        m_prev = m_scr[h:h + 1, 0:1]
        inter = b_col + m_prev
        m_tok = jnp.maximum(inter, jnp.max(log_d, axis=1, keepdims=True))
        d_mat = jnp.exp(log_d - m_tok)
        w_inter = jnp.exp(inter - m_tok)

        q = q_ref[:, h * dk:(h + 1) * dk] * (dk ** -0.5)
        k = k_ref[:, h * dk:(h + 1) * dk]
        qb = q.astype(BF16)
        vb = v_ref[:, h * dv:(h + 1) * dv].astype(BF16)
        s = lax.dot_general(qb, k.astype(BF16), (((1,), (1,)), ((), ())),
                            preferred_element_type=F32) * d_mat
        c_old = c_scr[h]
        n_old = n_scr[h:h + 1, :]
        num = (w_inter * jnp.dot(qb, c_old.astype(BF16), preferred_element_type=F32)
               + jnp.dot(s.astype(BF16), vb, preferred_element_type=F32))
        den = (w_inter * jnp.sum(q * n_old, axis=1, keepdims=True)
               + jnp.sum(s, axis=1, keepdims=True))
        hid = num / jnp.maximum(jnp.abs(den), jnp.exp(-m_tok))

        m_new = m_tok[length - 1:length, :]
        b_last = b_col[length - 1:length, :]
        w_c = jnp.exp(b_last + m_prev - m_new)
        w_s = jnp.exp((b_last - b_col) + ig_col - m_new)
        kw = k * w_s
        c_scr[h] = w_c * c_old + lax.dot_general(
            kw.astype(BF16), vb, (((0,), (0,)), ((), ())), preferred_element_type=F32)
        n_scr[h:h + 1, :] = w_c * n_old + jnp.sum(kw, axis=0, keepdims=True)
        m_scr[h:h + 1, :] = jnp.broadcast_to(m_new, (1, m_scr.shape[1]))

        normed = _rms(hid) * ghead_ref[:, h * dv:(h + 1) * dv]
        out_ref[:, h * dv:(h + 1) * dv] = (
            _sigmoid_tanh(o_ref[:, h * dv:(h + 1) * dv]) * normed).astype(out_ref.dtype)

    @pl.when(chunk == pl.num_programs(1) - 1)
    def _emit_state():
        c_out[...] = c_scr[...]
        n_out[...] = n_scr[...]
        m_out[...] = m_scr[...]


def _mlstm(proj, gates, g_head, state, *, batch, seq, cast=None):
    heads = MLSTM_HEADS
    d_mlstm = g_head.shape[-1]
    dv = d_mlstm // heads
    dk = dv // 2
    length = min(MLSTM_CHUNK, seq)
    nc = seq // length
    has_state = state is not None

    def rows(b, c):
        return b * nc + c

    in_specs = [
        pl.BlockSpec((length, heads * dk), lambda b, c: (rows(b, c), 0)),
        pl.BlockSpec((length, heads * dk), lambda b, c: (rows(b, c), 1)),
        pl.BlockSpec((length, d_mlstm), lambda b, c: (rows(b, c), 1)),
        pl.BlockSpec((length, d_mlstm), lambda b, c: (rows(b, c), 2)),
        pl.BlockSpec((length, LANES), lambda b, c: (rows(b, c), 0)),
        pl.BlockSpec((1, d_mlstm), lambda b, c: (0, 0)),
    ]
    args = [proj, proj, proj, proj, gates, g_head]
    if has_state:
        c0, n0, m0 = state
        in_specs += [
            pl.BlockSpec((None, heads, dk, dv), lambda b, c: (b, 0, 0, 0)),
            pl.BlockSpec((None, heads, dk), lambda b, c: (b, 0, 0)),
            pl.BlockSpec((None, heads, LANES), lambda b, c: (b, 0, 0)),
        ]
        args += [c0, n0, jnp.broadcast_to(m0[..., None], (batch, heads, LANES))]
    out_specs = [
        pl.BlockSpec((length, d_mlstm), lambda b, c: (rows(b, c), 0)),
        pl.BlockSpec((None, heads, dk, dv), lambda b, c: (b, 0, 0, 0)),
        pl.BlockSpec((None, heads, dk), lambda b, c: (b, 0, 0)),
        pl.BlockSpec((None, heads, LANES), lambda b, c: (b, 0, 0)),
    ]
    out_shape = [
        jax.ShapeDtypeStruct((batch * seq, d_mlstm), BF16),
        jax.ShapeDtypeStruct((batch, heads, dk, dv), F32),
        jax.ShapeDtypeStruct((batch, heads, dk), F32),
        jax.ShapeDtypeStruct((batch, heads, LANES), F32),
    ]
    if cast is not None:
        job = _CastJob(*cast, steps=batch * nc, step_of=rows)
        in_specs.append(job.spec)
        out_specs.append(job.spec)
        out_shape.append(job.out_shape)
        args.append(job.src)
    out, c1, n1, m1, *cast_out = pl.pallas_call(
        functools.partial(_mlstm_kernel, heads=heads, dk=dk, dv=dv, has_state=has_state,
                          has_cast=cast is not None),
        grid=(batch, nc),
        in_specs=in_specs,
        out_specs=out_specs,
        out_shape=out_shape,
        scratch_shapes=[
            pltpu.VMEM((heads, dk, dv), F32),
            pltpu.VMEM((heads, dk), F32),
            pltpu.VMEM((heads, LANES), F32),
        ],
        compiler_params=_params(("arbitrary", "arbitrary") if cast is not None else ("parallel", "arbitrary")),
        name="mlstm",
    )(*args)
    return (out, c1, n1, m1[..., 0], *cast_out)


def _rglru_kernel(*refs, reset_first, has_state, has_cast):
    refs = list(refs)
    if has_cast:
        _CastJob.run(refs.pop(8 + 2 * has_state), refs.pop(-5))
    if has_state:
        (x_ref, gr_ref, convw_ref, convb_ref, wcat_ref, bga_ref, bgx_ref, lam_ref, tail0_ref, h0_ref,
         out_ref, hlast_ref, tail_out_ref, xs_scr, a_scr, b_scr, h_scr) = refs
    else:
        (x_ref, gr_ref, convw_ref, convb_ref, wcat_ref, bga_ref, bgx_ref, lam_ref,
         out_ref, hlast_ref, tail_out_ref, xs_scr, a_scr, b_scr, h_scr) = refs
    t_idx = pl.program_id(2)
    nb, tt, cb = x_ref.shape
    rows = nb * tt
    halo = (CONV_WIDTH - 1) * nb

    nblk = cb // LANES
    blocks = [slice(blk * LANES, (blk + 1) * LANES) for blk in range(nblk)]

    @pl.when(t_idx == 0)
    def _init():
        for blk, lanes in enumerate(blocks):
            if has_state:
                xs_scr[blk, 0:halo, :] = tail0_ref[:, lanes]
                h_scr[blk] = h0_ref[:, lanes]
            else:
                xs_scr[blk, 0:halo, :] = jnp.zeros((halo, LANES), F32)
                h_scr[blk] = jnp.zeros((nb, LANES), F32)

    @pl.when(t_idx > 0)
    def _carry_history():
        xs_scr[:, 0:halo, :] = xs_scr[:, rows:rows + halo, :]

    row = lax.broadcasted_iota(jnp.int32, (rows, LANES), 0)
    first_token = (row < nb) & (t_idx == 0)
    for blk, lanes in enumerate(blocks):
        for s in range(nb):
            xs_scr[blk, pl.ds(halo + s, tt, stride=nb), :] = x_ref[s, :, lanes]
        xc = convb_ref[:, lanes]
        for tap in range(CONV_WIDTH):
            xc = xc + xs_scr[blk, tap * nb:tap * nb + rows, :] * convw_ref[tap:tap + 1, lanes]
        pre = jnp.dot(xc.astype(BF16), wcat_ref[blk], preferred_element_type=F32)
        r_gate = _sigmoid_tanh(pre[:, :LANES] + bga_ref[:, lanes])
        i_gate = _sigmoid_tanh(pre[:, LANES:] + bgx_ref[:, lanes])
        log_a = -RG_C * r_gate * _softplus(-lam_ref[:, lanes])
        a = jnp.exp(log_a)
        gap = -jnp.tanh(log_a) * (1.0 + a * a)
        mult = jnp.where(gap > 0.0, gap * lax.rsqrt(gap), 0.0)
        if reset_first:
            mult = jnp.where(first_token, 1.0, mult)
            a = jnp.where(first_token, 0.0, a)
        a_scr[blk] = a
        b_scr[blk] = mult * (i_gate * xc)

    def scan_step(s, h):
        rows_s = pl.ds(pl.multiple_of(s * nb, nb), nb)
        h = a_scr[:, rows_s, :] * h + b_scr[:, rows_s, :]
        b_scr[:, rows_s, :] = h
        return h

    h_scr[...] = lax.fori_loop(0, tt, scan_step, h_scr[...], unroll=8)

    for blk, lanes in enumerate(blocks):
        for s in range(nb):
            h_stream = b_scr[blk, pl.ds(s, tt, stride=nb), :]
            out_ref[s, :, lanes] = (h_stream * _gelu_tanh(gr_ref[s, :, lanes])).astype(out_ref.dtype)

    @pl.when(t_idx == pl.num_programs(2) - 1)
    def _emit_state():
        for blk, lanes in enumerate(blocks):
            hlast_ref[:, lanes] = h_scr[blk]
            tail_out_ref[:, lanes] = xs_scr[blk, rows:rows + halo, :]


def _rglru(proj3, rg, state, *, reset_first, xr_col, gr_col, cast=None):
    conv_w, conv_b, w_cat, b_ga, b_gx, lam = rg
    batch, seq, _ = proj3.shape
    d_rg = conv_b.shape[-1]
    nb = SUBLANES
    groups = batch // nb
    cb = RGLRU_CB
    tt = min(RGLRU_TT, seq)
    nt = seq // tt
    ncb = d_rg // cb
    xr_off = xr_col // cb
    gr_off = gr_col // cb
    halo = (CONV_WIDTH - 1) * nb
    has_state = state is not None

    def vec(rows):
        return pl.BlockSpec((rows, cb), lambda g, j, t: (0, j))

    in_specs = [
        pl.BlockSpec((nb, tt, cb), lambda g, j, t: (g, t, xr_off + j)),
        pl.BlockSpec((nb, tt, cb), lambda g, j, t: (g, t, gr_off + j)),
        vec(CONV_WIDTH), vec(1),
        pl.BlockSpec((cb // LANES, LANES, 2 * LANES), lambda g, j, t: (j, 0, 0)),
        vec(1), vec(1), vec(1),
    ]
    args = [proj3, proj3, conv_w, conv_b, w_cat, b_ga, b_gx, lam]
    if has_state:
        h0, conv0 = state
        tail0 = conv0.reshape(groups, nb, CONV_WIDTH - 1, d_rg).transpose(0, 2, 1, 3).reshape(groups, halo, d_rg)
        in_specs += [
            pl.BlockSpec((None, halo, cb), lambda g, j, t: (g, 0, j)),
            pl.BlockSpec((nb, cb), lambda g, j, t: (g, j)),
        ]
        args += [tail0, h0]
    out_specs = [
        pl.BlockSpec((nb, tt, cb), lambda g, j, t: (g, t, j)),
        pl.BlockSpec((nb, cb), lambda g, j, t: (g, j)),
        pl.BlockSpec((None, halo, cb), lambda g, j, t: (g, 0, j)),
    ]
    out_shape = [
        jax.ShapeDtypeStruct((batch, seq, d_rg), BF16),
        jax.ShapeDtypeStruct((batch, d_rg), F32),
        jax.ShapeDtypeStruct((groups, halo, d_rg), F32),
    ]
    if cast is not None:
        job = _CastJob(*cast, steps=groups * ncb * nt, step_of=lambda g, j, t: (g * ncb + j) * nt + t)
        in_specs.append(job.spec)
        out_specs.append(job.spec)
        out_shape.append(job.out_shape)
        args.append(job.src)
    out, h_last, tail, *cast_out = pl.pallas_call(
        functools.partial(_rglru_kernel, reset_first=reset_first, has_state=has_state,
                          has_cast=cast is not None),
        grid=(groups, ncb, nt),
        in_specs=in_specs,
        out_specs=out_specs,
        out_shape=out_shape,
        scratch_shapes=[
            pltpu.VMEM((cb // LANES, nb * tt + halo, LANES), F32),
            pltpu.VMEM((cb // LANES, nb * tt, LANES), F32),
            pltpu.VMEM((cb // LANES, nb * tt, LANES), F32),
            pltpu.VMEM((cb // LANES, nb, LANES), F32),
        ],
        compiler_params=_params(("arbitrary",) * 3 if cast is not None else ("parallel", "parallel", "arbitrary")),
        name="rglru",
    )(*args)
    conv1 = tail.reshape(groups, CONV_WIDTH - 1, nb, d_rg).transpose(0, 2, 1, 3).reshape(
        batch, CONV_WIDTH - 1, d_rg)
    return (out, h_last, conv1, *cast_out)


def _norm_rows(src_ref, fn, *, chunk):
    n = src_ref.shape[0] // chunk

    def body(i, carry):
        rows = pl.ds(pl.multiple_of(i * chunk, chunk), chunk)
        fn(rows, src_ref[rows, :])
        return carry

    lax.fori_loop(0, n, body, 0)


def _out_proj_kernel(a_ref, b_ref, w_ref, res_ref, g1_ref, g2_ref, x1_ref, u2_ref, *, nka):
    k = pl.program_id(1)

    @pl.when(k == 0)
    def _zero():
        x1_ref[...] = jnp.zeros_like(x1_ref)

    @pl.when(k < nka)
    def _from_a():
        x1_ref[...] += jnp.dot(a_ref[...], w_ref[...], preferred_element_type=F32)

    @pl.when(k >= nka)
    def _from_b():
        x1_ref[...] += jnp.dot(b_ref[...], w_ref[...], preferred_element_type=F32)

    @pl.when(k == pl.num_programs(1) - 1)
    def _epilogue():
        def fn(rows, y):
            x1 = res_ref[rows, :] + _rms(y) * g1_ref[...]
            x1_ref[rows, :] = x1
            u2_ref[rows, :] = (_rms(x1) * g2_ref[...]).astype(u2_ref.dtype)

        _norm_rows(x1_ref, fn, chunk=64)


def _out_proj(a, b, w, res, g1, g2, *, tm, tk):
    m, ka = a.shape
    kb = b.shape[1]
    n = w.shape[1]
    nka = ka // tk
    nk = nka + kb // tk
    return pl.pallas_call(
        functools.partial(_out_proj_kernel, nka=nka),
        grid=(m // tm, nk),
        in_specs=[
            pl.BlockSpec((tm, tk), lambda i, k: (i, jnp.minimum(k, nka - 1))),
            pl.BlockSpec((tm, tk), lambda i, k: (i, jnp.maximum(k - nka, 0))),
            pl.BlockSpec((tk, n), lambda i, k: (k, 0)),
            pl.BlockSpec((tm, n), lambda i, k: (i, 0)),
            pl.BlockSpec((1, n), lambda i, k: (0, 0)),
            pl.BlockSpec((1, n), lambda i, k: (0, 0)),
        ],
        out_specs=[
            pl.BlockSpec((tm, n), lambda i, k: (i, 0)),
            pl.BlockSpec((tm, n), lambda i, k: (i, 0)),
        ],
        out_shape=[jax.ShapeDtypeStruct((m, n), F32), jax.ShapeDtypeStruct((m, n), BF16)],
        compiler_params=_params(("parallel", "arbitrary")),
        name="out_proj",
    )(a, b, w, res, g1, g2)


def _ffn_kernel(u_ref, res_ref, g_ref, wg_hbm, wu_hbm, wd_hbm, o_ref, wg_buf, wu_buf, wd_buf, sem, *, nj):
    tf = wd_buf.shape[1]
    i = pl.program_id(0)
    first = i % 2
    second = 1 - first

    def copies(j, slot):
        cols = pl.ds(pl.multiple_of(j * tf, tf), tf)
        return (
            pltpu.make_async_copy(wg_hbm.at[:, cols], wg_buf.at[slot], sem.at[0, slot]),
            pltpu.make_async_copy(wu_hbm.at[:, cols], wu_buf.at[slot], sem.at[1, slot]),
            pltpu.make_async_copy(wd_hbm.at[cols, :], wd_buf.at[slot], sem.at[2, slot]),
        )

    def start(j, slot):
        for copy in copies(j, slot):
            copy.start()

    def wait(j, slot):
        for copy in copies(j, slot):
            copy.wait()

    def accumulate(slot):
        u = u_ref[...]
        gate = jnp.dot(u, wg_buf[slot], preferred_element_type=F32)
        up = jnp.dot(u, wu_buf[slot], preferred_element_type=F32)
        hidden = (gate * _sigmoid(gate) * up).astype(BF16)
        o_ref[...] += jnp.dot(hidden, wd_buf[slot], preferred_element_type=F32)

    @pl.when(i == 0)
    def _first_fetch():
        start(0, first)

    o_ref[...] = jnp.zeros_like(o_ref)

    def pair(p, carry):
        j = 2 * p
        start(j + 1, second)
        wait(j, first)
        accumulate(first)
        start(j + 2, first)
        wait(j + 1, second)
        accumulate(second)
        return carry

    lax.fori_loop(0, nj // 2, pair, 0)

    @pl.when(i + 1 < pl.num_programs(0))
    def _prefetch_next_row_tile():
        start(0, second)

    wait(nj - 1, first)
    accumulate(first)

    def fn(rows, y):
        o_ref[rows, :] = res_ref[rows, :] + _rms(y) * g_ref[...]

    _norm_rows(o_ref, fn, chunk=64)


def _ffn(u, wg, wu, wd, res, g, *, tm, tf):
    m, d = u.shape
    f = wg.shape[1]
    nj = f // tf
    assert nj % 2 == 1, "the slot schedule in _ffn_kernel is written for an odd number of column tiles"
    hbm = pl.BlockSpec(memory_space=pl.ANY)
    return pl.pallas_call(
        functools.partial(_ffn_kernel, nj=nj),
        grid=(m // tm,),
        in_specs=[
            pl.BlockSpec((tm, d), lambda i: (i, 0)),
            pl.BlockSpec((tm, d), lambda i: (i, 0)),
            pl.BlockSpec((1, d), lambda i: (0, 0)),
            hbm, hbm, hbm,
        ],
        out_specs=pl.BlockSpec((tm, d), lambda i: (i, 0)),
        out_shape=jax.ShapeDtypeStruct((m, d), F32),
        scratch_shapes=[
            pltpu.VMEM((2, d, tf), BF16),
            pltpu.VMEM((2, d, tf), BF16),
            pltpu.VMEM((2, tf, d), BF16),
            pltpu.SemaphoreType.DMA((3, 2)),
        ],
        compiler_params=_params(("arbitrary",)),
        name="ffn",
    )(u, res, g, wg, wu, wd)


def _prepare_weights(g_pre_mix, w_in, b_igate, b_fgate, g_mlstm_head, conv_w, conv_b, w_rg_a, b_rg_a,
                     w_rg_x, b_rg_x, rg_lambda, w_out, g_post_mix, g_pre_ffn, w_ffn_gate, w_ffn_up,
                     w_ffn_down, g_post_ffn):
    heads = MLSTM_HEADS
    d_mlstm = g_mlstm_head.shape[-1]
    n_dense = heads * (d_mlstm // heads // 2) * 2 + 2 * d_mlstm
    gate_lo, gate_hi = n_dense, n_dense + 2 * heads
    w_in_bf = w_in.astype(BF16)
    w_rg = w_in_bf[:, gate_hi:]
    w_gates = jnp.pad(w_in_bf[:, gate_lo:gate_hi], ((0, 0), (0, LANES - 2 * heads)))
    gate_bias = jnp.pad(jnp.concatenate([b_igate, b_fgate]), (0, LANES - 2 * heads))[None, :]
    row = lambda v: v[None, :].astype(F32)
    rg = (conv_w.astype(F32), row(conv_b),
          jnp.concatenate([w_rg_a, w_rg_x], axis=-1).astype(BF16),
          row(b_rg_a), row(b_rg_x), row(rg_lambda))
    return dict(
        g_pre_mix=row(g_pre_mix), w_in_bf=w_in_bf, n_dense=n_dense, w_rg=w_rg, w_gates=w_gates,
        gate_bias=gate_bias.astype(F32), g_head=row(g_mlstm_head), rg=rg, w_out=w_out,
        g_post_mix=row(g_post_mix), g_pre_ffn=row(g_pre_ffn), w_gate=w_ffn_gate,
        w_up=w_ffn_up, w_down=w_ffn_down, g_post_ffn=row(g_post_ffn))


def _layer(x, mlstm_state, rglru_state, w, *, reset_first):
    batch, seq, d = x.shape
    n = batch * seq
    x2 = x.reshape(n, d)
    d_rg = w["rg"][1].shape[-1]
    tm = min(n, IN_PROJ_TM)

    def job(name):
        return (w[name], CAST_BLOCK_ROWS[name]) if w[name].dtype != BF16 else None

    w = dict(w)
    u, gates = _norm_gates(x2, w["g_pre_mix"], w["w_gates"], w["gate_bias"], rows=NORM_ROWS)
    proj_dense, *cast = _matmul(u, w["w_in_bf"], ncols=w["n_dense"], tm=tm, tn=IN_PROJ_TN, out_dtype=F32,
                                name="in_proj_dense", cast=job("w_gate"))
    w["w_gate"], = cast or [w["w_gate"]]
    proj_rg, *cast = _matmul(u, w["w_rg"], ncols=2 * d_rg, tm=tm, tn=IN_PROJ_TN, out_dtype=F32,
                             name="in_proj_rg", cast=job("w_up"))
    w["w_up"], = cast or [w["w_up"]]
    out_a, c1, n1, m1, *cast = _mlstm(proj_dense, gates, w["g_head"], mlstm_state, batch=batch, seq=seq,
                                      cast=job("w_down"))
    w["w_down"], = cast or [w["w_down"]]
    out_b, h1, conv1, *cast = _rglru(proj_rg.reshape(batch, seq, 2 * d_rg), w["rg"], rglru_state,
                                     reset_first=reset_first, xr_col=0, gr_col=d_rg, cast=job("w_out"))
    w["w_out"], = cast or [w["w_out"]]
    x1, u2 = _out_proj(out_a, out_b.reshape(n, d_rg), w["w_out"], x2, w["g_post_mix"], w["g_pre_ffn"],
                       tm=min(n, ROW_RESIDENT_TM), tk=OUT_PROJ_TK)
    y = _ffn(u2, w["w_gate"], w["w_up"], w["w_down"], x1, w["g_post_ffn"],
             tm=min(n, ROW_RESIDENT_TM), tf=FFN_TF)
    return w, (y.reshape(batch, seq, d), c1, n1, m1, h1, conv1)


def kernel(x_prompt, x_sample, state_mlstm_C, state_mlstm_n, state_mlstm_m, state_rglru_h, state_rglru_conv, g_pre_mix, w_in, b_igate, b_fgate, g_mlstm_head, conv_w, conv_b, w_rg_a, b_rg_a, w_rg_x, b_rg_x, rg_lambda, w_out, g_post_mix, g_pre_ffn, w_ffn_gate, w_ffn_up, w_ffn_down, g_post_ffn):
    depth = w_in.shape[0]
    yp, ys = x_prompt, x_sample
    prompt_states, sample_states = [], []
    for l in range(depth):
        w = _prepare_weights(
            g_pre_mix[l], w_in[l], b_igate[l], b_fgate[l], g_mlstm_head[l], conv_w[l], conv_b[l],
            w_rg_a[l], b_rg_a[l], w_rg_x[l], b_rg_x[l], rg_lambda[l], w_out[l], g_post_mix[l],
            g_pre_ffn[l], w_ffn_gate[l], w_ffn_up[l], w_ffn_down[l], g_post_ffn[l])
        w, (yp, *st) = _layer(yp, None, None, w, reset_first=True)
        prompt_states.append(st)
        w, (ys, *st) = _layer(
            ys, (state_mlstm_C[l].astype(F32), state_mlstm_n[l].astype(F32), state_mlstm_m[l].astype(F32)),
            (state_rglru_h[l].astype(F32), state_rglru_conv[l].astype(F32)), w, reset_first=False)
        sample_states.append(st)
    stack = lambda states, i: jnp.stack([s[i] for s in states])
    return (yp, ys,
            *(stack(prompt_states, i) for i in range(5)),
            *(stack(sample_states, i) for i in range(5)))
```

```python
import functools

import jax
import jax.numpy as jnp
from jax import lax
from jax.experimental import pallas as pl
from jax.experimental.pallas import tpu as pltpu

F32 = jnp.float32
BF16 = jnp.bfloat16

NORM_EPS = 1e-6
MLSTM_HEADS = 4
RG_BLOCKS = 16
CONV_WIDTH = 4
RG_C = 8.0
MLSTM_CHUNK = 256

LANES = 128
SUBLANES = 8
VMEM_LIMIT_BYTES = 56 * 1024 * 1024

NORM_ROWS = 256
IN_PROJ_TM = 1024
IN_PROJ_TN = 1024
ROW_RESIDENT_TM = 512
OUT_PROJ_TK = 512
FFN_TF = 256
CAST_BLOCK_ROWS = {"w_gate": 64, "w_up": 64, "w_down": 256, "w_out": 128}
RGLRU_TT = 128
RGLRU_CB = 1024


def _params(semantics):
    return pltpu.CompilerParams(dimension_semantics=semantics, vmem_limit_bytes=VMEM_LIMIT_BYTES)


def _rms(y):
    return y * lax.rsqrt(jnp.mean(y * y, axis=-1, keepdims=True) + NORM_EPS)


def _log_sigmoid(x):
    return jnp.minimum(x, 0.0) - jnp.log1p(jnp.exp(-jnp.abs(x)))


def _softplus(x):
    return jnp.maximum(x, 0.0) + jnp.log1p(jnp.exp(-jnp.abs(x)))


def _sigmoid(x):
    return 1.0 / (1.0 + jnp.exp(-x))


def _sigmoid_tanh(x):
    return 0.5 * jnp.tanh(0.5 * x) + 0.5


def _gelu_tanh(x):
    return 0.5 * x * (1.0 + jnp.tanh(0.7978845608028654 * (x + 0.044715 * (x * x * x))))


def _norm_gates_kernel(x_ref, g_ref, wg_ref, bias_ref, u_ref, gates_ref, *, heads):
    u = (_rms(x_ref[...]) * g_ref[...]).astype(BF16)
    u_ref[...] = u
    gates = jnp.dot(u, wg_ref[...], preferred_element_type=F32) + bias_ref[...]
    lane = lax.broadcasted_iota(jnp.int32, gates.shape, 1)
    is_forget = (lane >= heads) & (lane < 2 * heads)
    gates_ref[...] = jnp.where(is_forget, _log_sigmoid(gates), gates)


def _norm_gates(x, g, w_gates, gate_bias, *, rows):
    n, d = x.shape
    return pl.pallas_call(
        functools.partial(_norm_gates_kernel, heads=MLSTM_HEADS),
        grid=(n // rows,),
        in_specs=[
            pl.BlockSpec((rows, d), lambda i: (i, 0)),
            pl.BlockSpec((1, d), lambda i: (0, 0)),
            pl.BlockSpec((d, LANES), lambda i: (0, 0)),
            pl.BlockSpec((1, LANES), lambda i: (0, 0)),
        ],
        out_specs=[
            pl.BlockSpec((rows, d), lambda i: (i, 0)),
            pl.BlockSpec((rows, LANES), lambda i: (i, 0)),
        ],
        out_shape=[jax.ShapeDtypeStruct((n, d), BF16), jax.ShapeDtypeStruct((n, LANES), F32)],
        compiler_params=_params(("parallel",)),
        name="norm_gates",
    )(x, g, w_gates, gate_bias)


class _CastJob:
    def __init__(self, src, block_rows, steps, step_of):
        rows, cols = src.shape
        assert rows % block_rows == 0 and rows // block_rows <= steps, (src.shape, block_rows, steps)
        last = rows // block_rows - 1
        self.src = src
        self.spec = pl.BlockSpec((block_rows, cols), lambda *ids: (jnp.minimum(step_of(*ids), last), 0))
        self.out_shape = jax.ShapeDtypeStruct(src.shape, BF16)

    @staticmethod
    def run(src_ref, dst_ref):
        dst_ref[...] = src_ref[...].astype(dst_ref.dtype)


def _matmul_kernel(a_ref, w_ref, *rest):
    if len(rest) == 3:
        cast_src, o_ref, cast_dst = rest
        _CastJob.run(cast_src, cast_dst)
    else:
        (o_ref,) = rest
    o_ref[...] = jnp.dot(a_ref[...], w_ref[...], preferred_element_type=F32).astype(o_ref.dtype)


def _matmul(a, w, *, ncols, tm, tn, out_dtype, name, cast=None):
    m, k = a.shape
    grid = (m // tm, ncols // tn)
    in_specs = [
        pl.BlockSpec((tm, k), lambda i, j: (i, 0)),
        pl.BlockSpec((k, tn), lambda i, j: (0, j)),
    ]
    out_specs = [pl.BlockSpec((tm, tn), lambda i, j: (i, j))]
    out_shape = [jax.ShapeDtypeStruct((m, ncols), out_dtype)]
    args = [a, w]
    if cast is not None:
        job = _CastJob(*cast, steps=grid[0] * grid[1], step_of=lambda i, j: i * grid[1] + j)
        in_specs.append(job.spec)
        out_specs.append(job.spec)
        out_shape.append(job.out_shape)
        args.append(job.src)
    outs = pl.pallas_call(
        _matmul_kernel,
        grid=grid,
        in_specs=in_specs,
        out_specs=out_specs,
        out_shape=out_shape,
        compiler_params=_params(("arbitrary", "arbitrary") if cast is not None else ("parallel", "arbitrary")),
        name=name,
    )(*args)
    return tuple(outs)


def _mlstm_kernel(*refs, heads, dk, dv, has_state, has_cast):
    refs = list(refs)
    c_scr, n_scr, m_scr = refs[-3:]
    del refs[-3:]
    if has_cast:
        _CastJob.run(refs.pop(6 + 3 * has_state), refs.pop())
    if has_state:
        (q_ref, k_ref, v_ref, o_ref, gates_ref, ghead_ref, c0_ref, n0_ref, m0_ref,
         out_ref, c_out, n_out, m_out) = refs
    else:
        q_ref, k_ref, v_ref, o_ref, gates_ref, ghead_ref, out_ref, c_out, n_out, m_out = refs
    chunk = pl.program_id(1)
    length = q_ref.shape[0]

    @pl.when(chunk == 0)
    def _init():
        if has_state:
            c_scr[...] = c0_ref[...]
            n_scr[...] = n0_ref[...]
            m_scr[...] = m0_ref[...]
        else:
            c_scr[...] = jnp.zeros_like(c_scr)
            n_scr[...] = jnp.zeros_like(n_scr)
            m_scr[...] = jnp.zeros_like(m_scr)

    row = lax.broadcasted_iota(jnp.int32, (length, length), 0)
    col = lax.broadcasted_iota(jnp.int32, (length, length), 1)
    causal = col <= row
    eye = col == row
    gates = gates_ref[...]

    for h in range(heads):
        ig_col = gates[:, h:h + 1]
        lf_col = gates[:, heads + h:heads + h + 1]
        b_row = jnp.sum(jnp.where(row <= col, lf_col, 0.0), axis=0, keepdims=True)
        b_col = jnp.sum(jnp.where(eye, b_row, 0.0), axis=1, keepdims=True)
        ig_row = jnp.sum(jnp.where(eye, ig_col, 0.0), axis=0, keepdims=True)
        log_d = jnp.where(causal, (b_col - b_row) + ig_row, -jnp.inf)
        m_prev = m_scr[h:h + 1, 0:1]
        inter = b_col + m_prev
        m_tok = jnp.maximum(inter, jnp.max(log_d, axis=1, keepdims=True))
        d_mat = jnp.exp(log_d - m_tok)
        w_inter = jnp.exp(inter - m_tok)

        q = q_ref[:, h * dk:(h + 1) * dk] * (dk ** -0.5)
        k = k_ref[:, h * dk:(h + 1) * dk]
        qb = q.astype(BF16)
        vb = v_ref[:, h * dv:(h + 1) * dv].astype(BF16)
        s = lax.dot_general(qb, k.astype(BF16), (((1,), (1,)), ((), ())),
                            preferred_element_type=F32) * d_mat
        c_old = c_scr[h]
        n_old = n_scr[h:h + 1, :]
        num = (w_inter * jnp.dot(qb, c_old.astype(BF16), preferred_element_type=F32)
               + jnp.dot(s.astype(BF16), vb, preferred_element_type=F32))
        den = (w_inter * jnp.sum(q * n_old, axis=1, keepdims=True)
               + jnp.sum(s, axis=1, keepdims=True))
        hid = num / jnp.maximum(jnp.abs(den), jnp.exp(-m_tok))

        m_new = m_tok[length - 1:length, :]
        b_last = b_col[length - 1:length, :]
        w_c = jnp.exp(b_last + m_prev - m_new)
        w_s = jnp.exp((b_last - b_col) + ig_col - m_new)
        kw = k * w_s
        c_scr[h] = w_c * c_old + lax.dot_general(
            kw.astype(BF16), vb, (((0,), (0,)), ((), ())), preferred_element_type=F32)
        n_scr[h:h + 1, :] = w_c * n_old + jnp.sum(kw, axis=0, keepdims=True)
        m_scr[h:h + 1, :] = jnp.broadcast_to(m_new, (1, m_scr.shape[1]))

        normed = _rms(hid) * ghead_ref[:, h * dv:(h + 1) * dv]
        out_ref[:, h * dv:(h + 1) * dv] = (
            _sigmoid_tanh(o_ref[:, h * dv:(h + 1) * dv]) * normed).astype(out_ref.dtype)

    @pl.when(chunk == pl.num_programs(1) - 1)
    def _emit_state():
        c_out[...] = c_scr[...]
        n_out[...] = n_scr[...]
        m_out[...] = m_scr[...]


def _mlstm(proj, gates, g_head, state, *, batch, seq, cast=None):
    heads = MLSTM_HEADS
    d_mlstm = g_head.shape[-1]
    dv = d_mlstm // heads
    dk = dv // 2
    length = min(MLSTM_CHUNK, seq)
    nc = seq // length
    has_state = state is not None

    def rows(b, c):
        return b * nc + c

    in_specs = [
        pl.BlockSpec((length, heads * dk), lambda b, c: (rows(b, c), 0)),
        pl.BlockSpec((length, heads * dk), lambda b, c: (rows(b, c), 1)),
        pl.BlockSpec((length, d_mlstm), lambda b, c: (rows(b, c), 1)),
        pl.BlockSpec((length, d_mlstm), lambda b, c: (rows(b, c), 2)),
        pl.BlockSpec((length, LANES), lambda b, c: (rows(b, c), 0)),
        pl.BlockSpec((1, d_mlstm), lambda b, c: (0, 0)),
    ]
    args = [proj, proj, proj, proj, gates, g_head]
    if has_state:
        c0, n0, m0 = state
        in_specs += [
            pl.BlockSpec((None, heads, dk, dv), lambda b, c: (b, 0, 0, 0)),
            pl.BlockSpec((None, heads, dk), lambda b, c: (b, 0, 0)),
            pl.BlockSpec((None, heads, LANES), lambda b, c: (b, 0, 0)),
        ]
        args += [c0, n0, jnp.broadcast_to(m0[..., None], (batch, heads, LANES))]
    out_specs = [
        pl.BlockSpec((length, d_mlstm), lambda b, c: (rows(b, c), 0)),
        pl.BlockSpec((None, heads, dk, dv), lambda b, c: (b, 0, 0, 0)),
        pl.BlockSpec((None, heads, dk), lambda b, c: (b, 0, 0)),
        pl.BlockSpec((None, heads, LANES), lambda b, c: (b, 0, 0)),
    ]
    out_shape = [
        jax.ShapeDtypeStruct((batch * seq, d_mlstm), BF16),
        jax.ShapeDtypeStruct((batch, heads, dk, dv), F32),
        jax.ShapeDtypeStruct((batch, heads, dk), F32),
        jax.ShapeDtypeStruct((batch, heads, LANES), F32),
    ]
    if cast is not None:
        job = _CastJob(*cast, steps=batch * nc, step_of=rows)
        in_specs.append(job.spec)
        out_specs.append(job.spec)
        out_shape.append(job.out_shape)
        args.append(job.src)
    out, c1, n1, m1, *cast_out = pl.pallas_call(
        functools.partial(_mlstm_kernel, heads=heads, dk=dk, dv=dv, has_state=has_state,
                          has_cast=cast is not None),
        grid=(batch, nc),
        in_specs=in_specs,
        out_specs=out_specs,
        out_shape=out_shape,
        scratch_shapes=[
            pltpu.VMEM((heads, dk, dv), F32),
            pltpu.VMEM((heads, dk), F32),
            pltpu.VMEM((heads, LANES), F32),
        ],
        compiler_params=_params(("arbitrary", "arbitrary") if cast is not None else ("parallel", "arbitrary")),
        name="mlstm",
    )(*args)
    return (out, c1, n1, m1[..., 0], *cast_out)


def _rglru_kernel(*refs, reset_first, has_state, has_cast):
    refs = list(refs)
    if has_cast:
        _CastJob.run(refs.pop(8 + 2 * has_state), refs.pop(-5))
    if has_state:
        (x_ref, gr_ref, convw_ref, convb_ref, wcat_ref, bga_ref, bgx_ref, lam_ref, tail0_ref, h0_ref,
         out_ref, hlast_ref, tail_out_ref, xs_scr, a_scr, b_scr, h_scr) = refs
    else:
        (x_ref, gr_ref, convw_ref, convb_ref, wcat_ref, bga_ref, bgx_ref, lam_ref,
         out_ref, hlast_ref, tail_out_ref, xs_scr, a_scr, b_scr, h_scr) = refs
    t_idx = pl.program_id(2)
    nb, tt, cb = x_ref.shape
    rows = nb * tt
    halo = (CONV_WIDTH - 1) * nb

    nblk = cb // LANES
    blocks = [slice(blk * LANES, (blk + 1) * LANES) for blk in range(nblk)]

    @pl.when(t_idx == 0)
    def _init():
        for blk, lanes in enumerate(blocks):
            if has_state:
                xs_scr[blk, 0:halo, :] = tail0_ref[:, lanes]
                h_scr[blk] = h0_ref[:, lanes]
            else:
                xs_scr[blk, 0:halo, :] = jnp.zeros((halo, LANES), F32)
                h_scr[blk] = jnp.zeros((nb, LANES), F32)

    @pl.when(t_idx > 0)
    def _carry_history():
        xs_scr[:, 0:halo, :] = xs_scr[:, rows:rows + halo, :]

    row = lax.broadcasted_iota(jnp.int32, (rows, LANES), 0)
    first_token = (row < nb) & (t_idx == 0)
    for blk, lanes in enumerate(blocks):
        for s in range(nb):
            xs_scr[blk, pl.ds(halo + s, tt, stride=nb), :] = x_ref[s, :, lanes]
        xc = convb_ref[:, lanes]
        for tap in range(CONV_WIDTH):
            xc = xc + xs_scr[blk, tap * nb:tap * nb + rows, :] * convw_ref[tap:tap + 1, lanes]
        pre = jnp.dot(xc.astype(BF16), wcat_ref[blk], preferred_element_type=F32)
        r_gate = _sigmoid_tanh(pre[:, :LANES] + bga_ref[:, lanes])
        i_gate = _sigmoid_tanh(pre[:, LANES:] + bgx_ref[:, lanes])
        log_a = -RG_C * r_gate * _softplus(-lam_ref[:, lanes])
        a = jnp.exp(log_a)
        gap = -jnp.tanh(log_a) * (1.0 + a * a)
        mult = jnp.where(gap > 0.0, gap * lax.rsqrt(gap), 0.0)
        if reset_first:
            mult = jnp.where(first_token, 1.0, mult)
            a = jnp.where(first_token, 0.0, a)
        a_scr[blk] = a
        b_scr[blk] = mult * (i_gate * xc)

    def scan_step(s, h):
        rows_s = pl.ds(pl.multiple_of(s * nb, nb), nb)
        h = a_scr[:, rows_s, :] * h + b_scr[:, rows_s, :]
        b_scr[:, rows_s, :] = h
        return h

    h_scr[...] = lax.fori_loop(0, tt, scan_step, h_scr[...], unroll=8)

    for blk, lanes in enumerate(blocks):
        for s in range(nb):
            h_stream = b_scr[blk, pl.ds(s, tt, stride=nb), :]
            out_ref[s, :, lanes] = (h_stream * _gelu_tanh(gr_ref[s, :, lanes])).astype(out_ref.dtype)

    @pl.when(t_idx == pl.num_programs(2) - 1)
    def _emit_state():
        for blk, lanes in enumerate(blocks):
            hlast_ref[:, lanes] = h_scr[blk]
            tail_out_ref[:, lanes] = xs_scr[blk, rows:rows + halo, :]


def _rglru(proj3, rg, state, *, reset_first, xr_col, gr_col, cast=None):
    conv_w, conv_b, w_cat, b_ga, b_gx, lam = rg
    batch, seq, _ = proj3.shape
    d_rg = conv_b.shape[-1]
    nb = SUBLANES
    groups = batch // nb
    cb = RGLRU_CB
    tt = min(RGLRU_TT, seq)
    nt = seq // tt
    ncb = d_rg // cb
    xr_off = xr_col // cb
    gr_off = gr_col // cb
    halo = (CONV_WIDTH - 1) * nb
    has_state = state is not None

    def vec(rows):
        return pl.BlockSpec((rows, cb), lambda g, j, t: (0, j))

    in_specs = [
        pl.BlockSpec((nb, tt, cb), lambda g, j, t: (g, t, xr_off + j)),
        pl.BlockSpec((nb, tt, cb), lambda g, j, t: (g, t, gr_off + j)),
        vec(CONV_WIDTH), vec(1),
        pl.BlockSpec((cb // LANES, LANES, 2 * LANES), lambda g, j, t: (j, 0, 0)),
        vec(1), vec(1), vec(1),
    ]
    args = [proj3, proj3, conv_w, conv_b, w_cat, b_ga, b_gx, lam]
    if has_state:
        h0, conv0 = state
        tail0 = conv0.reshape(groups, nb, CONV_WIDTH - 1, d_rg).transpose(0, 2, 1, 3).reshape(groups, halo, d_rg)
        in_specs += [
            pl.BlockSpec((None, halo, cb), lambda g, j, t: (g, 0, j)),
            pl.BlockSpec((nb, cb), lambda g, j, t: (g, j)),
        ]
        args += [tail0, h0]
    out_specs = [
        pl.BlockSpec((nb, tt, cb), lambda g, j, t: (g, t, j)),
        pl.BlockSpec((nb, cb), lambda g, j, t: (g, j)),
        pl.BlockSpec((None, halo, cb), lambda g, j, t: (g, 0, j)),
    ]
    out_shape = [
        jax.ShapeDtypeStruct((batch, seq, d_rg), BF16),
        jax.ShapeDtypeStruct((batch, d_rg), F32),
        jax.ShapeDtypeStruct((groups, halo, d_rg), F32),
    ]
    if cast is not None:
        job = _CastJob(*cast, steps=groups * ncb * nt, step_of=lambda g, j, t: (g * ncb + j) * nt + t)
        in_specs.append(job.spec)
        out_specs.append(job.spec)
        out_shape.append(job.out_shape)
        args.append(job.src)
    out, h_last, tail, *cast_out = pl.pallas_call(
        functools.partial(_rglru_kernel, reset_first=reset_first, has_state=has_state,
                          has_cast=cast is not None),
        grid=(groups, ncb, nt),
        in_specs=in_specs,
        out_specs=out_specs,
        out_shape=out_shape,
        scratch_shapes=[
            pltpu.VMEM((cb // LANES, nb * tt + halo, LANES), F32),
            pltpu.VMEM((cb // LANES, nb * tt, LANES), F32),
            pltpu.VMEM((cb // LANES, nb * tt, LANES), F32),
            pltpu.VMEM((cb // LANES, nb, LANES), F32),
        ],
        compiler_params=_params(("arbitrary",) * 3 if cast is not None else ("parallel", "parallel", "arbitrary")),
        name="rglru",
    )(*args)
    conv1 = tail.reshape(groups, CONV_WIDTH - 1, nb, d_rg).transpose(0, 2, 1, 3).reshape(
        batch, CONV_WIDTH - 1, d_rg)
    return (out, h_last, conv1, *cast_out)


def _norm_rows(src_ref, fn, *, chunk):
    n = src_ref.shape[0] // chunk

    def body(i, carry):
        rows = pl.ds(pl.multiple_of(i * chunk, chunk), chunk)
        fn(rows, src_ref[rows, :])
        return carry

    lax.fori_loop(0, n, body, 0)


def _out_proj_kernel(a_ref, b_ref, w_ref, res_ref, g1_ref, g2_ref, x1_ref, u2_ref, *, nka):
    k = pl.program_id(1)

    @pl.when(k == 0)
    def _zero():
        x1_ref[...] = jnp.zeros_like(x1_ref)

    @pl.when(k < nka)
    def _from_a():
        x1_ref[...] += jnp.dot(a_ref[...], w_ref[...], preferred_element_type=F32)

    @pl.when(k >= nka)
    def _from_b():
        x1_ref[...] += jnp.dot(b_ref[...], w_ref[...], preferred_element_type=F32)

    @pl.when(k == pl.num_programs(1) - 1)
    def _epilogue():
        def fn(rows, y):
            x1 = res_ref[rows, :] + _rms(y) * g1_ref[...]
            x1_ref[rows, :] = x1
            u2_ref[rows, :] = (_rms(x1) * g2_ref[...]).astype(u2_ref.dtype)

        _norm_rows(x1_ref, fn, chunk=64)


def _out_proj(a, b, w, res, g1, g2, *, tm, tk):
    m, ka = a.shape
    kb = b.shape[1]
    n = w.shape[1]
    nka = ka // tk
    nk = nka + kb // tk
    return pl.pallas_call(
        functools.partial(_out_proj_kernel, nka=nka),
        grid=(m // tm, nk),
        in_specs=[
            pl.BlockSpec((tm, tk), lambda i, k: (i, jnp.minimum(k, nka - 1))),
            pl.BlockSpec((tm, tk), lambda i, k: (i, jnp.maximum(k - nka, 0))),
            pl.BlockSpec((tk, n), lambda i, k: (k, 0)),
            pl.BlockSpec((tm, n), lambda i, k: (i, 0)),
            pl.BlockSpec((1, n), lambda i, k: (0, 0)),
            pl.BlockSpec((1, n), lambda i, k: (0, 0)),
        ],
        out_specs=[
            pl.BlockSpec((tm, n), lambda i, k: (i, 0)),
            pl.BlockSpec((tm, n), lambda i, k: (i, 0)),
        ],
        out_shape=[jax.ShapeDtypeStruct((m, n), F32), jax.ShapeDtypeStruct((m, n), BF16)],
        compiler_params=_params(("parallel", "arbitrary")),
        name="out_proj",
    )(a, b, w, res, g1, g2)


def _ffn_kernel(u_ref, wg_ref, wu_ref, wd_ref, res_ref, g_ref, o_ref):
    j = pl.program_id(1)

    @pl.when(j == 0)
    def _zero():
        o_ref[...] = jnp.zeros_like(o_ref)

    u = u_ref[...]
    gate = jnp.dot(u, wg_ref[...], preferred_element_type=F32)
    up = jnp.dot(u, wu_ref[...], preferred_element_type=F32)
    hidden = (gate * _sigmoid(gate) * up).astype(BF16)
    o_ref[...] += jnp.dot(hidden, wd_ref[...], preferred_element_type=F32)

    @pl.when(j == pl.num_programs(1) - 1)
    def _epilogue():
        def fn(rows, y):
            o_ref[rows, :] = res_ref[rows, :] + _rms(y) * g_ref[...]

        _norm_rows(o_ref, fn, chunk=64)


def _ffn(u, wg, wu, wd, res, g, *, tm, tf):
    m, d = u.shape
    f = wg.shape[1]
    return pl.pallas_call(
        _ffn_kernel,
        grid=(m // tm, f // tf),
        in_specs=[
            pl.BlockSpec((tm, d), lambda i, j: (i, 0)),
            pl.BlockSpec((d, tf), lambda i, j: (0, j)),
            pl.BlockSpec((d, tf), lambda i, j: (0, j)),
            pl.BlockSpec((tf, d), lambda i, j: (j, 0)),
            pl.BlockSpec((tm, d), lambda i, j: (i, 0)),
            pl.BlockSpec((1, d), lambda i, j: (0, 0)),
        ],
        out_specs=pl.BlockSpec((tm, d), lambda i, j: (i, 0)),
        out_shape=jax.ShapeDtypeStruct((m, d), F32),
        compiler_params=_params(("parallel", "arbitrary")),
        name="ffn",
    )(u, wg, wu, wd, res, g)


def _prepare_weights(g_pre_mix, w_in, b_igate, b_fgate, g_mlstm_head, conv_w, conv_b, w_rg_a, b_rg_a,
                     w_rg_x, b_rg_x, rg_lambda, w_out, g_post_mix, g_pre_ffn, w_ffn_gate, w_ffn_up,
                     w_ffn_down, g_post_ffn):
    heads = MLSTM_HEADS
    d_mlstm = g_mlstm_head.shape[-1]
    n_dense = heads * (d_mlstm // heads // 2) * 2 + 2 * d_mlstm
    gate_lo, gate_hi = n_dense, n_dense + 2 * heads
    w_in_bf = w_in.astype(BF16)
    w_rg = w_in_bf[:, gate_hi:]
    w_gates = jnp.pad(w_in_bf[:, gate_lo:gate_hi], ((0, 0), (0, LANES - 2 * heads)))
    gate_bias = jnp.pad(jnp.concatenate([b_igate, b_fgate]), (0, LANES - 2 * heads))[None, :]
    row = lambda v: v[None, :].astype(F32)
    rg = (conv_w.astype(F32), row(conv_b),
          jnp.concatenate([w_rg_a, w_rg_x], axis=-1).astype(BF16),
          row(b_rg_a), row(b_rg_x), row(rg_lambda))
    return dict(
        g_pre_mix=row(g_pre_mix), w_in_bf=w_in_bf, n_dense=n_dense, w_rg=w_rg, w_gates=w_gates,
        gate_bias=gate_bias.astype(F32), g_head=row(g_mlstm_head), rg=rg, w_out=w_out,
        g_post_mix=row(g_post_mix), g_pre_ffn=row(g_pre_ffn), w_gate=w_ffn_gate,
        w_up=w_ffn_up, w_down=w_ffn_down, g_post_ffn=row(g_post_ffn))


def _layer(x, mlstm_state, rglru_state, w, *, reset_first):
    batch, seq, d = x.shape
    n = batch * seq
    x2 = x.reshape(n, d)
    d_rg = w["rg"][1].shape[-1]
    tm = min(n, IN_PROJ_TM)

    def job(name):
        return (w[name], CAST_BLOCK_ROWS[name]) if w[name].dtype != BF16 else None

    w = dict(w)
    u, gates = _norm_gates(x2, w["g_pre_mix"], w["w_gates"], w["gate_bias"], rows=NORM_ROWS)
    proj_dense, *cast = _matmul(u, w["w_in_bf"], ncols=w["n_dense"], tm=tm, tn=IN_PROJ_TN, out_dtype=F32,
                                name="in_proj_dense", cast=job("w_gate"))
    w["w_gate"], = cast or [w["w_gate"]]
    proj_rg, *cast = _matmul(u, w["w_rg"], ncols=2 * d_rg, tm=tm, tn=IN_PROJ_TN, out_dtype=F32,
                             name="in_proj_rg", cast=job("w_up"))
    w["w_up"], = cast or [w["w_up"]]
    out_a, c1, n1, m1, *cast = _mlstm(proj_dense, gates, w["g_head"], mlstm_state, batch=batch, seq=seq,
                                      cast=job("w_down"))
    w["w_down"], = cast or [w["w_down"]]
    out_b, h1, conv1, *cast = _rglru(proj_rg.reshape(batch, seq, 2 * d_rg), w["rg"], rglru_state,
                                     reset_first=reset_first, xr_col=0, gr_col=d_rg, cast=job("w_out"))
    w["w_out"], = cast or [w["w_out"]]
    x1, u2 = _out_proj(out_a, out_b.reshape(n, d_rg), w["w_out"], x2, w["g_post_mix"], w["g_pre_ffn"],
                       tm=min(n, ROW_RESIDENT_TM), tk=OUT_PROJ_TK)
    y = _ffn(u2, w["w_gate"], w["w_up"], w["w_down"], x1, w["g_post_ffn"],
             tm=min(n, ROW_RESIDENT_TM), tf=FFN_TF)
    return w, (y.reshape(batch, seq, d), c1, n1, m1, h1, conv1)


def kernel(x_prompt, x_sample, state_mlstm_C, state_mlstm_n, state_mlstm_m, state_rglru_h, state_rglru_conv, g_pre_mix, w_in, b_igate, b_fgate, g_mlstm_head, conv_w, conv_b, w_rg_a, b_rg_a, w_rg_x, b_rg_x, rg_lambda, w_out, g_post_mix, g_pre_ffn, w_ffn_gate, w_ffn_up, w_ffn_down, g_post_ffn):
    depth = w_in.shape[0]
    yp, ys = x_prompt, x_sample
    prompt_states, sample_states = [], []
    for l in range(depth):
        w = _prepare_weights(
            g_pre_mix[l], w_in[l], b_igate[l], b_fgate[l], g_mlstm_head[l], conv_w[l], conv_b[l],
            w_rg_a[l], b_rg_a[l], w_rg_x[l], b_rg_x[l], rg_lambda[l], w_out[l], g_post_mix[l],
            g_pre_ffn[l], w_ffn_gate[l], w_ffn_up[l], w_ffn_down[l], g_post_ffn[l])
        w, (yp, *st) = _layer(yp, None, None, w, reset_first=True)
        prompt_states.append(st)
        w, (ys, *st) = _layer(
            ys, (state_mlstm_C[l].astype(F32), state_mlstm_n[l].astype(F32), state_mlstm_m[l].astype(F32)),
            (state_rglru_h[l].astype(F32), state_rglru_conv[l].astype(F32)), w, reset_first=False)
        sample_states.append(st)
    stack = lambda states, i: jnp.stack([s[i] for s in states])
    return (yp, ys,
            *(stack(prompt_states, i) for i in range(5)),
            *(stack(sample_states, i) for i in range(5)))
```

```python
import functools

import jax
import jax.numpy as jnp
from jax import lax
from jax.experimental import pallas as pl
from jax.experimental.pallas import tpu as pltpu

F32 = jnp.float32
BF16 = jnp.bfloat16

NORM_EPS = 1e-6
MLSTM_HEADS = 4
CONV_WIDTH = 4
RG_C = 8.0
MLSTM_CHUNK = 256

LANES = 128
SUBLANES = 8
VMEM_LIMIT_BYTES = 56 * 1024 * 1024

NORM_ROWS = 256
IN_PROJ_TM = 1024
IN_PROJ_TN = 1024
ROW_RESIDENT_TM = 512
OUT_PROJ_TK = 512
FFN_TF = 256
CAST_BLOCK_ROWS = {"w_gate": 64, "w_up": 64, "w_down": 256, "w_out": 128}
RGLRU_TT = 128
RGLRU_CB = 1024


def _params(semantics):
    return pltpu.CompilerParams(dimension_semantics=semantics, vmem_limit_bytes=VMEM_LIMIT_BYTES)


def _rms(y):
    return y * lax.rsqrt(jnp.mean(y * y, axis=-1, keepdims=True) + NORM_EPS)


def _log_sigmoid(x):
    return jnp.minimum(x, 0.0) - jnp.log1p(jnp.exp(-jnp.abs(x)))


def _softplus(x):
    return jnp.maximum(x, 0.0) + jnp.log1p(jnp.exp(-jnp.abs(x)))


def _sigmoid(x):
    return 1.0 / (1.0 + jnp.exp(-x))


def _sigmoid_tanh(x):
    return 0.5 * jnp.tanh(0.5 * x) + 0.5


def _gelu_tanh(x):
    return 0.5 * x * (1.0 + jnp.tanh(0.7978845608028654 * (x + 0.044715 * (x * x * x))))


def _norm_gates_kernel(x_ref, g_ref, wg_ref, bias_ref, u_ref, gates_ref, *, heads):
    u = (_rms(x_ref[...]) * g_ref[...]).astype(BF16)
    u_ref[...] = u
    gates = jnp.dot(u, wg_ref[...], preferred_element_type=F32) + bias_ref[...]
    lane = lax.broadcasted_iota(jnp.int32, gates.shape, 1)
    is_forget = (lane >= heads) & (lane < 2 * heads)
    gates_ref[...] = jnp.where(is_forget, _log_sigmoid(gates), gates)


def _norm_gates(x, g, w_gates, gate_bias, *, rows):
    n, d = x.shape
    return pl.pallas_call(
        functools.partial(_norm_gates_kernel, heads=MLSTM_HEADS),
        grid=(n // rows,),
        in_specs=[
            pl.BlockSpec((rows, d), lambda i: (i, 0)),
            pl.BlockSpec((1, d), lambda i: (0, 0)),
            pl.BlockSpec((d, LANES), lambda i: (0, 0)),
            pl.BlockSpec((1, LANES), lambda i: (0, 0)),
        ],
        out_specs=[
            pl.BlockSpec((rows, d), lambda i: (i, 0)),
            pl.BlockSpec((rows, LANES), lambda i: (i, 0)),
        ],
        out_shape=[jax.ShapeDtypeStruct((n, d), BF16), jax.ShapeDtypeStruct((n, LANES), F32)],
        compiler_params=_params(("parallel",)),
        name="norm_gates",
    )(x, g, w_gates, gate_bias)


class _CastJob:
    def __init__(self, src, block_rows, steps, step_of):
        rows, cols = src.shape
        assert rows % block_rows == 0 and rows // block_rows <= steps, (src.shape, block_rows, steps)
        last = rows // block_rows - 1
        self.src = src
        self.spec = pl.BlockSpec((block_rows, cols), lambda *ids: (jnp.minimum(step_of(*ids), last), 0))
        self.out_shape = jax.ShapeDtypeStruct(src.shape, BF16)

    @staticmethod
    def run(src_ref, dst_ref):
        dst_ref[...] = src_ref[...].astype(dst_ref.dtype)


def _matmul_kernel(a_ref, w_ref, *rest):
    if len(rest) == 3:
        cast_src, o_ref, cast_dst = rest
        _CastJob.run(cast_src, cast_dst)
    else:
        (o_ref,) = rest
    o_ref[...] = jnp.dot(a_ref[...], w_ref[...], preferred_element_type=F32).astype(o_ref.dtype)


def _matmul(a, w, *, ncols, tm, tn, out_dtype, name, cast=None):
    m, k = a.shape
    grid = (m // tm, ncols // tn)
    in_specs = [
        pl.BlockSpec((tm, k), lambda i, j: (i, 0)),
        pl.BlockSpec((k, tn), lambda i, j: (0, j)),
    ]
    out_specs = [pl.BlockSpec((tm, tn), lambda i, j: (i, j))]
    out_shape = [jax.ShapeDtypeStruct((m, ncols), out_dtype)]
    args = [a, w]
    if cast is not None:
        job = _CastJob(*cast, steps=grid[0] * grid[1], step_of=lambda i, j: i * grid[1] + j)
        in_specs.append(job.spec)
        out_specs.append(job.spec)
        out_shape.append(job.out_shape)
        args.append(job.src)
    outs = pl.pallas_call(
        _matmul_kernel,
        grid=grid,
        in_specs=in_specs,
        out_specs=out_specs,
        out_shape=out_shape,
        compiler_params=_params(("arbitrary", "arbitrary") if cast is not None else ("parallel", "arbitrary")),
        name=name,
    )(*args)
    return tuple(outs)


def _mlstm_kernel(*refs, heads, dk, dv, has_state, has_cast):
    refs = list(refs)
    c_scr, n_scr, m_scr = refs[-3:]
    del refs[-3:]
    if has_cast:
        _CastJob.run(refs.pop(6 + 3 * has_state), refs.pop())
    if has_state:
        (q_ref, k_ref, v_ref, o_ref, gates_ref, ghead_ref, c0_ref, n0_ref, m0_ref,
         out_ref, c_out, n_out, m_out) = refs
    else:
        q_ref, k_ref, v_ref, o_ref, gates_ref, ghead_ref, out_ref, c_out, n_out, m_out = refs
    chunk = pl.program_id(1)
    length = q_ref.shape[0]

    @pl.when(chunk == 0)
    def _init():
        if has_state:
            c_scr[...] = c0_ref[...]
            n_scr[...] = n0_ref[...]
            m_scr[...] = m0_ref[...]
        else:
            c_scr[...] = jnp.zeros_like(c_scr)
            n_scr[...] = jnp.zeros_like(n_scr)
            m_scr[...] = jnp.zeros_like(m_scr)

    row = lax.broadcasted_iota(jnp.int32, (length, length), 0)
    col = lax.broadcasted_iota(jnp.int32, (length, length), 1)
    causal = col <= row
    eye = col == row
    gates = gates_ref[...]

    for h in range(heads):
        ig_col = gates[:, h:h + 1]
        lf_col = gates[:, heads + h:heads + h + 1]
        b_row = jnp.sum(jnp.where(row <= col, lf_col, 0.0), axis=0, keepdims=True)
        b_col = jnp.sum(jnp.where(eye, b_row, 0.0), axis=1, keepdims=True)
        ig_row = jnp.sum(jnp.where(eye, ig_col, 0.0), axis=0, keepdims=True)
        log_d = jnp.where(causal, (b_col - b_row) + ig_row, -jnp.inf)
        m_prev = m_scr[h:h + 1, 0:1]
        inter = b_col + m_prev
        m_tok = jnp.maximum(inter, jnp.max(log_d, axis=1, keepdims=True))
        d_mat = jnp.exp(log_d - m_tok)
        w_inter = jnp.exp(inter - m_tok)

        q = q_ref[:, h * dk:(h + 1) * dk] * (dk ** -0.5)
        k = k_ref[:, h * dk:(h + 1) * dk]
        qb = q.astype(BF16)
        vb = v_ref[:, h * dv:(h + 1) * dv].astype(BF16)
        s = lax.dot_general(qb, k.astype(BF16), (((1,), (1,)), ((), ())),
                            preferred_element_type=F32) * d_mat
        c_old = c_scr[h]
        n_old = n_scr[h:h + 1, :]
        num = (w_inter * jnp.dot(qb, c_old.astype(BF16), preferred_element_type=F32)
               + jnp.dot(s.astype(BF16), vb, preferred_element_type=F32))
        den = (w_inter * jnp.sum(q * n_old, axis=1, keepdims=True)
               + jnp.sum(s, axis=1, keepdims=True))
        hid = num / jnp.maximum(jnp.abs(den), jnp.exp(-m_tok))

        m_new = m_tok[length - 1:length, :]
        b_last = b_col[length - 1:length, :]
        w_c = jnp.exp(b_last + m_prev - m_new)
        w_s = jnp.exp((b_last - b_col) + ig_col - m_new)
        kw = k * w_s
        c_scr[h] = w_c * c_old + lax.dot_general(
            kw.astype(BF16), vb, (((0,), (0,)), ((), ())), preferred_element_type=F32)
        n_scr[h:h + 1, :] = w_c * n_old + jnp.sum(kw, axis=0, keepdims=True)
        m_scr[h:h + 1, :] = jnp.broadcast_to(m_new, (1, m_scr.shape[1]))

        normed = _rms(hid) * ghead_ref[:, h * dv:(h + 1) * dv]
        out_ref[:, h * dv:(h + 1) * dv] = (
            _sigmoid_tanh(o_ref[:, h * dv:(h + 1) * dv]) * normed).astype(out_ref.dtype)

    @pl.when(chunk == pl.num_programs(1) - 1)
    def _emit_state():
        c_out[...] = c_scr[...]
        n_out[...] = n_scr[...]
        m_out[...] = m_scr[...]


def _mlstm(proj, gates, g_head, state, *, batch, seq, cast=None):
    heads = MLSTM_HEADS
    d_mlstm = g_head.shape[-1]
    dv = d_mlstm // heads
    dk = dv // 2
    length = min(MLSTM_CHUNK, seq)
    nc = seq // length
    has_state = state is not None

    def rows(b, c):
        return b * nc + c

    in_specs = [
        pl.BlockSpec((length, heads * dk), lambda b, c: (rows(b, c), 0)),
        pl.BlockSpec((length, heads * dk), lambda b, c: (rows(b, c), 1)),
        pl.BlockSpec((length, d_mlstm), lambda b, c: (rows(b, c), 1)),
        pl.BlockSpec((length, d_mlstm), lambda b, c: (rows(b, c), 2)),
        pl.BlockSpec((length, LANES), lambda b, c: (rows(b, c), 0)),
        pl.BlockSpec((1, d_mlstm), lambda b, c: (0, 0)),
    ]
    args = [proj, proj, proj, proj, gates, g_head]
    if has_state:
        c0, n0, m0 = state
        in_specs += [
            pl.BlockSpec((None, heads, dk, dv), lambda b, c: (b, 0, 0, 0)),
            pl.BlockSpec((None, heads, dk), lambda b, c: (b, 0, 0)),
            pl.BlockSpec((None, heads, LANES), lambda b, c: (b, 0, 0)),
        ]
        args += [c0, n0, jnp.broadcast_to(m0[..., None], (batch, heads, LANES))]
    out_specs = [
        pl.BlockSpec((length, d_mlstm), lambda b, c: (rows(b, c), 0)),
        pl.BlockSpec((None, heads, dk, dv), lambda b, c: (b, 0, 0, 0)),
        pl.BlockSpec((None, heads, dk), lambda b, c: (b, 0, 0)),
        pl.BlockSpec((None, heads, LANES), lambda b, c: (b, 0, 0)),
    ]
    out_shape = [
        jax.ShapeDtypeStruct((batch * seq, d_mlstm), BF16),
        jax.ShapeDtypeStruct((batch, heads, dk, dv), F32),
        jax.ShapeDtypeStruct((batch, heads, dk), F32),
        jax.ShapeDtypeStruct((batch, heads, LANES), F32),
    ]
    if cast is not None:
        job = _CastJob(*cast, steps=batch * nc, step_of=rows)
        in_specs.append(job.spec)
        out_specs.append(job.spec)
        out_shape.append(job.out_shape)
        args.append(job.src)
    out, c1, n1, m1, *cast_out = pl.pallas_call(
        functools.partial(_mlstm_kernel, heads=heads, dk=dk, dv=dv, has_state=has_state,
                          has_cast=cast is not None),
        grid=(batch, nc),
        in_specs=in_specs,
        out_specs=out_specs,
        out_shape=out_shape,
        scratch_shapes=[
            pltpu.VMEM((heads, dk, dv), F32),
            pltpu.VMEM((heads, dk), F32),
            pltpu.VMEM((heads, LANES), F32),
        ],
        compiler_params=_params(("arbitrary", "arbitrary") if cast is not None else ("parallel", "arbitrary")),
        name="mlstm",
    )(*args)
    return (out, c1, n1, m1[..., 0], *cast_out)


def _rglru_kernel(*refs, reset_first, has_state, has_cast):
    refs = list(refs)
    if has_cast:
        _CastJob.run(refs.pop(8 + 2 * has_state), refs.pop(-5))
    if has_state:
        (x_ref, gr_ref, convw_ref, convb_ref, wcat_ref, bga_ref, bgx_ref, lam_ref, tail0_ref, h0_ref,
         out_ref, hlast_ref, tail_out_ref, xs_scr, a_scr, b_scr, h_scr) = refs
    else:
        (x_ref, gr_ref, convw_ref, convb_ref, wcat_ref, bga_ref, bgx_ref, lam_ref,
         out_ref, hlast_ref, tail_out_ref, xs_scr, a_scr, b_scr, h_scr) = refs
    t_idx = pl.program_id(2)
    nb, tt, cb = x_ref.shape
    rows = nb * tt
    halo = (CONV_WIDTH - 1) * nb

    nblk = cb // LANES
    blocks = [slice(blk * LANES, (blk + 1) * LANES) for blk in range(nblk)]

    @pl.when(t_idx == 0)
    def _init():
        for blk, lanes in enumerate(blocks):
            if has_state:
                xs_scr[blk, 0:halo, :] = tail0_ref[:, lanes]
                h_scr[blk] = h0_ref[:, lanes]
            else:
                xs_scr[blk, 0:halo, :] = jnp.zeros((halo, LANES), F32)
                h_scr[blk] = jnp.zeros((nb, LANES), F32)

    @pl.when(t_idx > 0)
    def _carry_history():
        xs_scr[:, 0:halo, :] = xs_scr[:, rows:rows + halo, :]

    row = lax.broadcasted_iota(jnp.int32, (rows, LANES), 0)
    first_token = (row < nb) & (t_idx == 0)
    for blk, lanes in enumerate(blocks):
        for s in range(nb):
            xs_scr[blk, pl.ds(halo + s, tt, stride=nb), :] = x_ref[s, :, lanes]
        xc = convb_ref[:, lanes]
        for tap in range(CONV_WIDTH):
            xc = xc + xs_scr[blk, tap * nb:tap * nb + rows, :] * convw_ref[tap:tap + 1, lanes]
        pre = jnp.dot(xc.astype(BF16), wcat_ref[blk], preferred_element_type=F32)
        r_gate = _sigmoid_tanh(pre[:, :LANES] + bga_ref[:, lanes])
        i_gate = _sigmoid_tanh(pre[:, LANES:] + bgx_ref[:, lanes])
        log_a = -RG_C * r_gate * _softplus(-lam_ref[:, lanes])
        a = jnp.exp(log_a)
        gap = -jnp.tanh(log_a) * (1.0 + a * a)
        mult = jnp.where(gap > 0.0, gap * lax.rsqrt(gap), 0.0)
        if reset_first:
            mult = jnp.where(first_token, 1.0, mult)
            a = jnp.where(first_token, 0.0, a)
        a_scr[blk] = a
        b_scr[blk] = mult * (i_gate * xc)

    def scan_step(s, h):
        rows_s = pl.ds(pl.multiple_of(s * nb, nb), nb)
        h = a_scr[:, rows_s, :] * h + b_scr[:, rows_s, :]
        b_scr[:, rows_s, :] = h
        return h

    h_scr[...] = lax.fori_loop(0, tt, scan_step, h_scr[...], unroll=8)

    for blk, lanes in enumerate(blocks):
        for s in range(nb):
            h_stream = b_scr[blk, pl.ds(s, tt, stride=nb), :]
            out_ref[s, :, lanes] = (h_stream * _gelu_tanh(gr_ref[s, :, lanes])).astype(out_ref.dtype)

    @pl.when(t_idx == pl.num_programs(2) - 1)
    def _emit_state():
        for blk, lanes in enumerate(blocks):
            hlast_ref[:, lanes] = h_scr[blk]
            tail_out_ref[:, lanes] = xs_scr[blk, rows:rows + halo, :]


def _rglru(proj3, rg, state, *, reset_first, xr_col, gr_col, cast=None):
    conv_w, conv_b, w_cat, b_ga, b_gx, lam = rg
    batch, seq, _ = proj3.shape
    d_rg = conv_b.shape[-1]
    nb = SUBLANES
    groups = batch // nb
    cb = RGLRU_CB
    tt = min(RGLRU_TT, seq)
    nt = seq // tt
    ncb = d_rg // cb
    xr_off = xr_col // cb
    gr_off = gr_col // cb
    halo = (CONV_WIDTH - 1) * nb
    has_state = state is not None

    def vec(rows):
        return pl.BlockSpec((rows, cb), lambda g, j, t: (0, j))

    in_specs = [
        pl.BlockSpec((nb, tt, cb), lambda g, j, t: (g, t, xr_off + j)),
        pl.BlockSpec((nb, tt, cb), lambda g, j, t: (g, t, gr_off + j)),
        vec(CONV_WIDTH), vec(1),
        pl.BlockSpec((cb // LANES, LANES, 2 * LANES), lambda g, j, t: (j, 0, 0)),
        vec(1), vec(1), vec(1),
    ]
    args = [proj3, proj3, conv_w, conv_b, w_cat, b_ga, b_gx, lam]
    if has_state:
        h0, conv0 = state
        tail0 = conv0.reshape(groups, nb, CONV_WIDTH - 1, d_rg).transpose(0, 2, 1, 3).reshape(groups, halo, d_rg)
        in_specs += [
            pl.BlockSpec((None, halo, cb), lambda g, j, t: (g, 0, j)),
            pl.BlockSpec((nb, cb), lambda g, j, t: (g, j)),
        ]
        args += [tail0, h0]
    out_specs = [
        pl.BlockSpec((nb, tt, cb), lambda g, j, t: (g, t, j)),
        pl.BlockSpec((nb, cb), lambda g, j, t: (g, j)),
        pl.BlockSpec((None, halo, cb), lambda g, j, t: (g, 0, j)),
    ]
    out_shape = [
        jax.ShapeDtypeStruct((batch, seq, d_rg), BF16),
        jax.ShapeDtypeStruct((batch, d_rg), F32),
        jax.ShapeDtypeStruct((groups, halo, d_rg), F32),
    ]
    if cast is not None:
        job = _CastJob(*cast, steps=groups * ncb * nt, step_of=lambda g, j, t: (g * ncb + j) * nt + t)
        in_specs.append(job.spec)
        out_specs.append(job.spec)
        out_shape.append(job.out_shape)
        args.append(job.src)
    out, h_last, tail, *cast_out = pl.pallas_call(
        functools.partial(_rglru_kernel, reset_first=reset_first, has_state=has_state,
                          has_cast=cast is not None),
        grid=(groups, ncb, nt),
        in_specs=in_specs,
        out_specs=out_specs,
        out_shape=out_shape,
        scratch_shapes=[
            pltpu.VMEM((cb // LANES, nb * tt + halo, LANES), F32),
            pltpu.VMEM((cb // LANES, nb * tt, LANES), F32),
            pltpu.VMEM((cb // LANES, nb * tt, LANES), F32),
            pltpu.VMEM((cb // LANES, nb, LANES), F32),
        ],
        compiler_params=_params(("arbitrary",) * 3 if cast is not None else ("parallel", "parallel", "arbitrary")),
        name="rglru",
    )(*args)
    conv1 = tail.reshape(groups, CONV_WIDTH - 1, nb, d_rg).transpose(0, 2, 1, 3).reshape(
        batch, CONV_WIDTH - 1, d_rg)
    return (out, h_last, conv1, *cast_out)


def _norm_rows(src_ref, fn, *, chunk):
    n = src_ref.shape[0] // chunk

    def body(i, carry):
        rows = pl.ds(pl.multiple_of(i * chunk, chunk), chunk)
        fn(rows, src_ref[rows, :])
        return carry

    lax.fori_loop(0, n, body, 0)


def _out_proj_kernel(a_ref, b_ref, w_ref, res_ref, g1_ref, g2_ref, x1_ref, u2_ref, *, nka):
    k = pl.program_id(1)

    @pl.when(k == 0)
    def _zero():
        x1_ref[...] = jnp.zeros_like(x1_ref)

    @pl.when(k < nka)
    def _from_a():
        x1_ref[...] += jnp.dot(a_ref[...], w_ref[...], preferred_element_type=F32)

    @pl.when(k >= nka)
    def _from_b():
        x1_ref[...] += jnp.dot(b_ref[...], w_ref[...], preferred_element_type=F32)

    @pl.when(k == pl.num_programs(1) - 1)
    def _epilogue():
        def fn(rows, y):
            x1 = res_ref[rows, :] + _rms(y) * g1_ref[...]
            x1_ref[rows, :] = x1
            u2_ref[rows, :] = (_rms(x1) * g2_ref[...]).astype(u2_ref.dtype)

        _norm_rows(x1_ref, fn, chunk=64)


def _out_proj(a, b, w, res, g1, g2, *, tm, tk):
    m, ka = a.shape
    kb = b.shape[1]
    n = w.shape[1]
    nka = ka // tk
    nk = nka + kb // tk
    return pl.pallas_call(
        functools.partial(_out_proj_kernel, nka=nka),
        grid=(m // tm, nk),
        in_specs=[
            pl.BlockSpec((tm, tk), lambda i, k: (i, jnp.minimum(k, nka - 1))),
            pl.BlockSpec((tm, tk), lambda i, k: (i, jnp.maximum(k - nka, 0))),
            pl.BlockSpec((tk, n), lambda i, k: (k, 0)),
            pl.BlockSpec((tm, n), lambda i, k: (i, 0)),
            pl.BlockSpec((1, n), lambda i, k: (0, 0)),
            pl.BlockSpec((1, n), lambda i, k: (0, 0)),
        ],
        out_specs=[
            pl.BlockSpec((tm, n), lambda i, k: (i, 0)),
            pl.BlockSpec((tm, n), lambda i, k: (i, 0)),
        ],
        out_shape=[jax.ShapeDtypeStruct((m, n), F32), jax.ShapeDtypeStruct((m, n), BF16)],
        compiler_params=_params(("parallel", "arbitrary")),
        name="out_proj",
    )(a, b, w, res, g1, g2)


def _ffn_kernel(u_ref, wg_ref, wu_ref, wd_ref, res_ref, g_ref, o_ref):
    j = pl.program_id(1)

    @pl.when(j == 0)
    def _zero():
        o_ref[...] = jnp.zeros_like(o_ref)

    u = u_ref[...]
    gate = jnp.dot(u, wg_ref[...], preferred_element_type=F32)
    up = jnp.dot(u, wu_ref[...], preferred_element_type=F32)
    hidden = (gate * _sigmoid(gate) * up).astype(BF16)
    o_ref[...] += jnp.dot(hidden, wd_ref[...], preferred_element_type=F32)

    @pl.when(j == pl.num_programs(1) - 1)
    def _epilogue():
        def fn(rows, y):
            o_ref[rows, :] = res_ref[rows, :] + _rms(y) * g_ref[...]

        _norm_rows(o_ref, fn, chunk=64)


def _ffn(u, wg, wu, wd, res, g, *, tm, tf):
    m, d = u.shape
    f = wg.shape[1]
    return pl.pallas_call(
        _ffn_kernel,
        grid=(m // tm, f // tf),
        in_specs=[
            pl.BlockSpec((tm, d), lambda i, j: (i, 0)),
            pl.BlockSpec((d, tf), lambda i, j: (0, j)),
            pl.BlockSpec((d, tf), lambda i, j: (0, j)),
            pl.BlockSpec((tf, d), lambda i, j: (j, 0)),
            pl.BlockSpec((tm, d), lambda i, j: (i, 0)),
            pl.BlockSpec((1, d), lambda i, j: (0, 0)),
        ],
        out_specs=pl.BlockSpec((tm, d), lambda i, j: (i, 0)),
        out_shape=jax.ShapeDtypeStruct((m, d), F32),
        compiler_params=_params(("parallel", "arbitrary")),
        name="ffn",
    )(u, wg, wu, wd, res, g)


def _prepare_weights(g_pre_mix, w_in, b_igate, b_fgate, g_mlstm_head, conv_w, conv_b, w_rg_a, b_rg_a,
                     w_rg_x, b_rg_x, rg_lambda, w_out, g_post_mix, g_pre_ffn, w_ffn_gate, w_ffn_up,
                     w_ffn_down, g_post_ffn):
    heads = MLSTM_HEADS
    d_mlstm = g_mlstm_head.shape[-1]
    n_dense = heads * (d_mlstm // heads // 2) * 2 + 2 * d_mlstm
    gate_lo, gate_hi = n_dense, n_dense + 2 * heads
    w_in_bf = w_in.astype(BF16)
    w_rg = w_in_bf[:, gate_hi:]
    w_gates = jnp.pad(w_in_bf[:, gate_lo:gate_hi], ((0, 0), (0, LANES - 2 * heads)))
    gate_bias = jnp.pad(jnp.concatenate([b_igate, b_fgate]), (0, LANES - 2 * heads))[None, :]
    row = lambda v: v[None, :].astype(F32)
    rg = (conv_w.astype(F32), row(conv_b),
          jnp.concatenate([w_rg_a, w_rg_x], axis=-1).astype(BF16),
          row(b_rg_a), row(b_rg_x), row(rg_lambda))
    return dict(
        g_pre_mix=row(g_pre_mix), w_in_bf=w_in_bf, n_dense=n_dense, w_rg=w_rg, w_gates=w_gates,
        gate_bias=gate_bias.astype(F32), g_head=row(g_mlstm_head), rg=rg, w_out=w_out,
        g_post_mix=row(g_post_mix), g_pre_ffn=row(g_pre_ffn), w_gate=w_ffn_gate,
        w_up=w_ffn_up, w_down=w_ffn_down, g_post_ffn=row(g_post_ffn))


def _layer(x, mlstm_state, rglru_state, w, *, reset_first):
    batch, seq, d = x.shape
    n = batch * seq
    x2 = x.reshape(n, d)
    d_rg = w["rg"][1].shape[-1]
    tm = min(n, IN_PROJ_TM)

    def job(name):
        return (w[name], CAST_BLOCK_ROWS[name]) if w[name].dtype != BF16 else None

    w = dict(w)
    u, gates = _norm_gates(x2, w["g_pre_mix"], w["w_gates"], w["gate_bias"], rows=NORM_ROWS)
    proj_dense, *cast = _matmul(u, w["w_in_bf"], ncols=w["n_dense"], tm=tm, tn=IN_PROJ_TN, out_dtype=F32,
                                name="in_proj_dense", cast=job("w_gate"))
    w["w_gate"], = cast or [w["w_gate"]]
    proj_rg, *cast = _matmul(u, w["w_rg"], ncols=2 * d_rg, tm=tm, tn=IN_PROJ_TN, out_dtype=F32,
                             name="in_proj_rg", cast=job("w_up"))
    w["w_up"], = cast or [w["w_up"]]
    out_a, c1, n1, m1, *cast = _mlstm(proj_dense, gates, w["g_head"], mlstm_state, batch=batch, seq=seq,
                                      cast=job("w_down"))
    w["w_down"], = cast or [w["w_down"]]
    out_b, h1, conv1, *cast = _rglru(proj_rg.reshape(batch, seq, 2 * d_rg), w["rg"], rglru_state,
                                     reset_first=reset_first, xr_col=0, gr_col=d_rg, cast=job("w_out"))
    w["w_out"], = cast or [w["w_out"]]
    x1, u2 = _out_proj(out_a, out_b.reshape(n, d_rg), w["w_out"], x2, w["g_post_mix"], w["g_pre_ffn"],
                       tm=min(n, ROW_RESIDENT_TM), tk=OUT_PROJ_TK)
    y = _ffn(u2, w["w_gate"], w["w_up"], w["w_down"], x1, w["g_post_ffn"],
             tm=min(n, ROW_RESIDENT_TM), tf=FFN_TF)
    return w, (y.reshape(batch, seq, d), c1, n1, m1, h1, conv1)


def kernel(x_prompt, x_sample, state_mlstm_C, state_mlstm_n, state_mlstm_m, state_rglru_h, state_rglru_conv, g_pre_mix, w_in, b_igate, b_fgate, g_mlstm_head, conv_w, conv_b, w_rg_a, b_rg_a, w_rg_x, b_rg_x, rg_lambda, w_out, g_post_mix, g_pre_ffn, w_ffn_gate, w_ffn_up, w_ffn_down, g_post_ffn):
    depth = w_in.shape[0]
    yp, ys = x_prompt, x_sample
    prompt_states, sample_states = [], []
    for l in range(depth):
        w = _prepare_weights(
            g_pre_mix[l], w_in[l], b_igate[l], b_fgate[l], g_mlstm_head[l], conv_w[l], conv_b[l],
            w_rg_a[l], b_rg_a[l], w_rg_x[l], b_rg_x[l], rg_lambda[l], w_out[l], g_post_mix[l],
            g_pre_ffn[l], w_ffn_gate[l], w_ffn_up[l], w_ffn_down[l], g_post_ffn[l])
        w, (yp, *st) = _layer(yp, None, None, w, reset_first=True)
        prompt_states.append(st)
        w, (ys, *st) = _layer(
            ys, (state_mlstm_C[l].astype(F32), state_mlstm_n[l].astype(F32), state_mlstm_m[l].astype(F32)),
            (state_rglru_h[l].astype(F32), state_rglru_conv[l].astype(F32)), w, reset_first=False)
        sample_states.append(st)
    stack = lambda states, i: jnp.stack([s[i] for s in states])
    return (yp, ys,
            *(stack(prompt_states, i) for i in range(5)),
            *(stack(sample_states, i) for i in range(5)))
```
